```python
import math
import jax, jax.numpy as jnp
from jax import lax
import numpy as np

D_MODEL = 1024
BATCH = 8
SEQ = 4096
DEPTH = 2

MLA_HEADS = 6
MLA_NOPE = 64
MLA_ROPE = 32
MLA_V = 64
MLA_Q_RANK = 256
MLA_KV_RANK = 128
ROPE_THETA = 10000.0
Q_BLOCK = 128
MLA_WIDTH = MLA_HEADS * MLA_V

HY_GROUPS = 6
HY_GROUP_W = 64
HY_WIDTH = HY_GROUPS * HY_GROUP_W
HY_ORDER = 2
HY_SHORT = 3
HY_BANDS = 8
HY_EMB = 1 + 2 * HY_BANDS
HY_FFN = 64
HY_FILTER_SCALE = 0.05
HY_FAST_DECAY = 0.3
HY_SLOW_DECAY = 1.5
HY_TARGET = 1e-2

NA_HEADS = 4
NA_HEAD_DIM = 64
NA_WIDTH = NA_HEADS * NA_HEAD_DIM
GRID_W = 64
NA_KH = 8
NA_KW = 16

D_MIX = MLA_WIDTH + HY_WIDTH + NA_WIDTH
IN_SIZES = (MLA_Q_RANK, MLA_KV_RANK, MLA_ROPE, 3 * HY_WIDTH, NA_WIDTH, NA_WIDTH, NA_WIDTH)
IN_COLS = MLA_Q_RANK + MLA_KV_RANK + MLA_ROPE + 3 * HY_WIDTH + 3 * NA_WIDTH
D_FF = ((8 * D_MODEL // 3 + 255) // 256) * 256
NORM_EPS = 1e-6

kernel_name = "hymba_style_mla_hyena_natten_encoder"


def rms_norm(x, g):
    xf = x.astype(jnp.float32)
    y = xf * lax.rsqrt(jnp.mean(xf * xf, axis=-1, keepdims=True) + NORM_EPS)
    return (y * g.astype(jnp.float32)).astype(x.dtype)


def split_cols(a, sizes):
    idx = np.cumsum(np.array(sizes))[:-1].tolist()
    return jnp.split(a, idx, axis=-1)


def rope_tables(S):
    pos = jnp.arange(S, dtype=jnp.float32)
    inv = ROPE_THETA ** (-jnp.arange(0, MLA_ROPE, 2, dtype=jnp.float32) / MLA_ROPE)
    ang = pos[:, None] * inv[None, :]
    return jnp.cos(ang), jnp.sin(ang)


def apply_rope(x, cos, sin):
    xf = x.astype(jnp.float32)
    half = xf.shape[-1] // 2
    x1, x2 = xf[..., :half], xf[..., half:]
    return jnp.concatenate([x1 * cos - x2 * sin, x1 * sin + x2 * cos], axis=-1).astype(x.dtype)


def mla_mixer(c_q, c_kv, k_pe, q_norm_g, w_uq, kv_norm_g, w_ukv):
    B, S, _ = c_q.shape
    q = (rms_norm(c_q, q_norm_g) @ w_uq).reshape(B, S, MLA_HEADS, MLA_NOPE + MLA_ROPE)
    kv = (rms_norm(c_kv, kv_norm_g) @ w_ukv).reshape(B, S, MLA_HEADS, MLA_NOPE + MLA_V)
    q_nope, q_pe = q[..., :MLA_NOPE], q[..., MLA_NOPE:]
    k_nope, v = kv[..., :MLA_NOPE], kv[..., MLA_NOPE:]
    cos, sin = rope_tables(S)
    q_pe = apply_rope(q_pe, cos[:, None, :], sin[:, None, :])
    k_pe = apply_rope(k_pe, cos, sin)
    scale = 1.0 / math.sqrt(MLA_NOPE + MLA_ROPE)
    nb = S // Q_BLOCK
    qn = q_nope.reshape(B, nb, Q_BLOCK, MLA_HEADS, MLA_NOPE).transpose(1, 0, 2, 3, 4)
    qp = q_pe.reshape(B, nb, Q_BLOCK, MLA_HEADS, MLA_ROPE).transpose(1, 0, 2, 3, 4)

    def block(args):
        qn_b, qp_b = args
        s = (jnp.einsum('bqhd,bkhd->bhqk', qn_b, k_nope)
             + jnp.einsum('bqhr,bkr->bhqk', qp_b, k_pe))
        p = jax.nn.softmax(s.astype(jnp.float32) * scale, axis=-1).astype(v.dtype)
        return jnp.einsum('bhqk,bkhd->bqhd', p, v)

    o = lax.map(block, (qn, qp))
    return o.transpose(1, 0, 2, 3, 4).reshape(B, S, MLA_WIDTH)


def hyena_filters(L, w1, b1, f1, w2, b2, f2, w3):
    t_idx = jnp.arange(L, dtype=jnp.float32)[:, None]
    t_norm = jnp.linspace(0.0, 1.0, L, dtype=jnp.float32)[:, None]
    bands = jnp.linspace(1e-4, HY_BANDS - 1, HY_BANDS, dtype=jnp.float32)[None, :]
    ang = 2.0 * math.pi * t_idx * bands / L
    z = jnp.concatenate([t_norm, jnp.cos(ang), jnp.sin(ang)], axis=-1)
    f32 = jnp.float32
    h = jnp.sin(f1.astype(f32) * (z @ w1.astype(f32) + b1.astype(f32)))
    h = jnp.sin(f2.astype(f32) * (h @ w2.astype(f32) + b2.astype(f32)))
    h = (h @ w3.astype(f32)).reshape(L, HY_ORDER, 2, HY_WIDTH)
    deltas = jnp.linspace(math.log(HY_TARGET) / HY_SLOW_DECAY,
                          math.log(HY_TARGET) / HY_FAST_DECAY, HY_WIDTH, dtype=jnp.float32)
    decay = jnp.exp(-t_norm * jnp.abs(deltas)[None, :])
    return h * decay[:, None, None, :]


def bidir_long_conv(u, h_f, h_b, skip):
    L = u.shape[1]
    n = 2 * L
    Hf = jnp.fft.rfft(h_f, n=n, axis=0)[None]
    Hb = jnp.fft.rfft(h_b, n=n, axis=0)[None]
    y_f = jnp.fft.irfft(jnp.fft.rfft(u, n=n, axis=1) * Hf, n=n, axis=1)[:, :L]
    ur = u[:, ::-1]
    y_b = jnp.fft.irfft(jnp.fft.rfft(ur, n=n, axis=1) * Hb, n=n, axis=1)[:, :L][:, ::-1]
    return y_f + y_b + skip[None, None, :] * u


def hyena_mixer(xh, conv_w, conv_b, w1, b1, f1, w2, b2, f2, w3, skip):
    B, L, _ = xh.shape
    xp = jnp.pad(xh, ((0, 0), (1, 1), (0, 0)))
    uc = conv_w[0] * xp[:, :-2] + conv_w[1] * xp[:, 1:-1] + conv_w[2] * xp[:, 2:] + conv_b
    v, x1, x2 = jnp.split(uc.astype(jnp.float32), 3, axis=-1)
    h = hyena_filters(L, w1, b1, f1, w2, b2, f2, w3)
    sk = skip.astype(jnp.float32)
    z = v
    for o, gate in enumerate((x1, x2)):
        z = gate * bidir_long_conv(z, h[:, o, 0], h[:, o, 1], sk[o])
    return z.astype(xh.dtype)


def natten_mixer(q, k, v, rpb):
    B, S, _ = q.shape
    R = S // GRID_W
    KH = min(NA_KH, R)
    KW = NA_KW
    qg = q.reshape(B, R, GRID_W, NA_HEADS, NA_HEAD_DIM)
    kg = k.reshape(B, R, GRID_W, NA_HEADS, NA_HEAD_DIM)
    vg = v.reshape(B, R, GRID_W, NA_HEADS, NA_HEAD_DIM)
    r = jnp.arange(R)
    row_idx = jnp.clip(r - KH // 2, 0, R - KH)[:, None] + jnp.arange(KH)[None, :]
    c = jnp.arange(GRID_W)
    col_idx = jnp.clip(c - KW // 2, 0, GRID_W - KW)[:, None] + jnp.arange(KW)[None, :]
    k_rows = kg[:, row_idx]
    v_rows = vg[:, row_idx]
    onehot = (col_idx[:, :, None] == c[None, None, :]).astype(q.dtype)
    s_rows = jnp.einsum('brwhd,brkvhd->brhwkv', qg, k_rows)
    s = jnp.einsum('brhwkv,wjv->brhwkj', s_rows, onehot)
    dr = row_idx - r[:, None] + (NA_KH - 1)
    dc = col_idx - c[:, None] + (NA_KW - 1)
    bias = rpb[:, dr[:, None, :, None], dc[None, :, None, :]]
    bias = bias.transpose(1, 0, 2, 3, 4)[None]
    logits = s.astype(jnp.float32) * (1.0 / math.sqrt(NA_HEAD_DIM)) + bias.astype(jnp.float32)
    shp = logits.shape
    p = jax.nn.softmax(logits.reshape(shp[:-2] + (KH * KW,)), axis=-1).reshape(shp).astype(v.dtype)
    p_rows = jnp.einsum('brhwkj,wjv->brhwkv', p, onehot)
    o = jnp.einsum('brhwkv,brkvhd->brwhd', p_rows, v_rows)
    return o.reshape(B, S, NA_WIDTH)


def setup_inputs(seed: int = 0) -> dict:
    key = jax.random.key(seed)
    ks = iter(jax.random.split(key, 32))
    f32 = jnp.float32

    def w(shape, fan_in, scale=1.0):
        return jax.random.normal(next(ks), shape, f32) * (scale * fan_in ** -0.5)

    def gain(shape):
        return jnp.ones(shape, f32) + 0.02 * jax.random.normal(next(ks), shape, f32)

    def small(shape, s=0.02):
        return s * jax.random.normal(next(ks), shape, f32)

    return {
        "x": jax.random.normal(next(ks), (BATCH, SEQ, D_MODEL), f32),
        "norm1_g": gain((DEPTH, D_MODEL)),
        "w_in": w((DEPTH, D_MODEL, IN_COLS), D_MODEL),
        "mla_q_norm_g": gain((DEPTH, MLA_Q_RANK)),
        "mla_w_uq": w((DEPTH, MLA_Q_RANK, MLA_HEADS * (MLA_NOPE + MLA_ROPE)), MLA_Q_RANK),
        "mla_kv_norm_g": gain((DEPTH, MLA_KV_RANK)),
        "mla_w_ukv": w((DEPTH, MLA_KV_RANK, MLA_HEADS * (MLA_NOPE + MLA_V)), MLA_KV_RANK),
        "hy_conv_w": w((DEPTH, HY_SHORT, 3 * HY_WIDTH), HY_SHORT),
        "hy_conv_b": small((DEPTH, 3 * HY_WIDTH)),
        "hy_filt_w1": w((DEPTH, HY_EMB, HY_FFN), HY_EMB),
        "hy_filt_b1": small((DEPTH, HY_FFN)),
        "hy_filt_freq1": gain((DEPTH, HY_FFN)),
        "hy_filt_w2": w((DEPTH, HY_FFN, HY_FFN), HY_FFN),
        "hy_filt_b2": small((DEPTH, HY_FFN)),
        "hy_filt_freq2": gain((DEPTH, HY_FFN)),
        "hy_filt_w3": w((DEPTH, HY_FFN, HY_ORDER * 2 * HY_WIDTH), HY_FFN, HY_FILTER_SCALE),
        "hy_skip": small((DEPTH, HY_ORDER, HY_WIDTH), 0.1),
        "na_rpb": small((DEPTH, NA_HEADS, 2 * NA_KH - 1, 2 * NA_KW - 1)),
        "mix_norm_g": gain((DEPTH, D_MIX)),
        "w_out": w((DEPTH, D_MIX, D_MODEL), D_MIX),
        "norm2_g": gain((DEPTH, D_MODEL)),
        "ffn_w_gate": w((DEPTH, D_MODEL, D_FF), D_MODEL),
        "ffn_w_up": w((DEPTH, D_MODEL, D_FF), D_MODEL),
        "ffn_w_down": w((DEPTH, D_FF, D_MODEL), D_FF),
        "final_norm_g": gain((D_MODEL,)),
    }


def reference(x, norm1_g, w_in, mla_q_norm_g, mla_w_uq, mla_kv_norm_g, mla_w_ukv,
              hy_conv_w, hy_conv_b, hy_filt_w1, hy_filt_b1, hy_filt_freq1, hy_filt_w2,
              hy_filt_b2, hy_filt_freq2, hy_filt_w3, hy_skip, na_rpb, mix_norm_g, w_out,
              norm2_g, ffn_w_gate, ffn_w_up, ffn_w_down, final_norm_g):
    for l in range(DEPTH):
        h = rms_norm(x, norm1_g[l])
        proj = h @ w_in[l]
        c_q, c_kv, k_pe, hy_in, na_q, na_k, na_v = split_cols(proj, IN_SIZES)
        y_a = mla_mixer(c_q, c_kv, k_pe, mla_q_norm_g[l], mla_w_uq[l],
                        mla_kv_norm_g[l], mla_w_ukv[l])
        y_b = hyena_mixer(hy_in, hy_conv_w[l], hy_conv_b[l], hy_filt_w1[l], hy_filt_b1[l],
                          hy_filt_freq1[l], hy_filt_w2[l], hy_filt_b2[l], hy_filt_freq2[l],
                          hy_filt_w3[l], hy_skip[l])
        y_c = natten_mixer(na_q, na_k, na_v, na_rpb[l])
        g = mix_norm_g[l]
        y = jnp.concatenate([
            rms_norm(y_a, g[:MLA_WIDTH]),
            rms_norm(y_b, g[MLA_WIDTH:MLA_WIDTH + HY_WIDTH]),
            rms_norm(y_c, g[MLA_WIDTH + HY_WIDTH:]),
        ], axis=-1)
        x = x + y @ w_out[l]
        h2 = rms_norm(x, norm2_g[l])
        x = x + (jax.nn.silu(h2 @ ffn_w_gate[l]) * (h2 @ ffn_w_up[l])) @ ffn_w_down[l]
    return rms_norm(x, final_norm_g)
```

```python
import functools
import math

import jax
import jax.numpy as jnp
from jax import lax
from jax.experimental import pallas as pl
from jax.experimental.pallas import tpu as pltpu

F32 = jnp.float32
BF16 = jnp.bfloat16

NORM_EPS = 1e-6
MLA_HEADS = 6
MLA_NOPE = 64
MLA_ROPE = 32
MLA_V = 64
MLA_Q_RANK = 256
MLA_KV_RANK = 128
ROPE_THETA = 10000.0
HY_WIDTH = 384
HY_ORDER = 2
HY_BANDS = 8
HY_EMB = 1 + 2 * HY_BANDS
HY_FFN = 64
HY_FAST_DECAY = 0.3
HY_SLOW_DECAY = 1.5
HY_TARGET = 1e-2
NA_HEADS = 4
NA_HEAD_DIM = 64
NA_WIDTH = NA_HEADS * NA_HEAD_DIM
GRID_W = 64
NA_KH = 8
NA_KW = 16
MLA_WIDTH = MLA_HEADS * MLA_V

LANE = 128
HEAD_PAD = 128
VMEM_LIMIT = 56 * 1024 * 1024
MASK_VALUE = -1e30


def _params(sem, vmem=VMEM_LIMIT):
    return pltpu.CompilerParams(dimension_semantics=sem, vmem_limit_bytes=vmem)


def _rms(x, g):
    return x * lax.rsqrt(jnp.mean(x * x, axis=-1, keepdims=True) + NORM_EPS) * g


def _const_spec(shape):
    nd = len(shape)
    return pl.BlockSpec(shape, lambda *_: (0,) * nd)


def _dft_table_kernel(c_ref, sp_ref, st_ref, ec_ref, es_ref, *, L, tk):
    i = pl.program_id(0)
    n = 2 * L
    w = 2.0 * math.pi / n

    @pl.when(i == 0)
    def _():
        r = lax.broadcasted_iota(jnp.int32, (tk, L), 0)
        t = lax.broadcasted_iota(jnp.int32, (tk, L), 1)
        ang = ((r * t) & (n - 1)).astype(F32) * w
        ec_ref[...] = jnp.cos(ang)
        es_ref[...] = jnp.sin(ang)

    k0 = i * tk
    t1 = lax.broadcasted_iota(jnp.int32, (1, L), 1)
    ang0 = ((k0 * t1) & (n - 1)).astype(F32) * w
    ca = jnp.cos(ang0)
    sa = jnp.sin(ang0)
    ec = ec_ref[...]
    es = es_ref[...]
    cv = ca * ec - sa * es
    sv = sa * ec + ca * es
    rows = k0 + lax.broadcasted_iota(jnp.int32, (tk, L), 0)
    cols = lax.broadcasted_iota(jnp.int32, (tk, L), 1)
    sgn_c = (1 - 2 * (cols & 1)).astype(F32)
    sgn_r = (1 - 2 * (rows & 1)).astype(F32)
    c_ref[...] = cv.astype(BF16)
    sp_ref[...] = jnp.where(rows == 0, sgn_c, sv).astype(BF16)
    st_ref[...] = jnp.where(cols == 0, sgn_r, sv).astype(BF16)


def _dft_tables(L):
    assert L & (L - 1) == 0, "sequence length must be a power of two"
    tk = min(128, L)
    spec = pl.BlockSpec((tk, L), lambda i: (i, 0))
    shp = jax.ShapeDtypeStruct((L, L), BF16)
    return pl.pallas_call(
        functools.partial(_dft_table_kernel, L=L, tk=tk),
        grid=(L // tk,),
        out_specs=[spec, spec, spec],
        out_shape=[shp, shp, shp],
        scratch_shapes=[pltpu.VMEM((tk, L), F32), pltpu.VMEM((tk, L), F32)],
        compiler_params=_params(("arbitrary",)),
        name="dft_tables",
    )()


def _inproj_kernel(x_ref, g1_ref, wlat_ref, why_ref, wna_ref, gq_ref, wq_ref, wqs_ref,
                   gkv_ref, wk_ref, wvt_ref, cq_ref, sq_ref, ck_ref, sk_ref,
                   q_ref, k_ref, vt_ref, hy_ref, naq_ref, nak_ref, nav_ref):
    x = x_ref[0]
    h = _rms(x, g1_ref[...]).astype(BF16)
    lat = jnp.dot(h, wlat_ref[...], preferred_element_type=F32)
    hy_ref[...] = jnp.dot(h, why_ref[...], preferred_element_type=F32)
    na = jnp.dot(h, wna_ref[...], preferred_element_type=F32)
    naq_ref[0] = (na[:, :NA_WIDTH] * (1.0 / math.sqrt(NA_HEAD_DIM))).astype(BF16)
    nak_ref[0] = na[:, NA_WIDTH:2 * NA_WIDTH].astype(BF16)
    nav_ref[0] = na[:, 2 * NA_WIDTH:].astype(BF16)

    c_q = lat[:, :MLA_Q_RANK]
    c_kv = lat[:, MLA_Q_RANK:MLA_Q_RANK + MLA_KV_RANK]
    o = MLA_Q_RANK + MLA_KV_RANK
    kpe = lat[:, o:o + HEAD_PAD]
    kpe_sw = lat[:, o + HEAD_PAD:o + 2 * HEAD_PAD]

    cqn = _rms(c_q, gq_ref[...]).astype(BF16)
    qf = jnp.dot(cqn, wq_ref[...], preferred_element_type=F32)
    qs = jnp.dot(cqn, wqs_ref[...], preferred_element_type=F32)
    cq = cq_ref[...]
    sq = sq_ref[...]
    ckvn = _rms(c_kv, gkv_ref[...])
    kf = jnp.dot(ckvn.astype(BF16), wk_ref[...], preferred_element_type=F32)
    kpe_r = kpe * ck_ref[...] + kpe_sw * sk_ref[...]
    for hd in range(MLA_HEADS):
        sl = slice(hd * HEAD_PAD, (hd + 1) * HEAD_PAD)
        q_ref[0, :, sl] = (qf[:, sl] * cq + qs[:, sl] * sq).astype(BF16)
        k_ref[0, :, sl] = (kf[:, sl] + kpe_r).astype(BF16)
    vt_ref[0] = jnp.dot(wvt_ref[...], ckvn.T.astype(BF16),
                        preferred_element_type=F32).astype(BF16)


def _inproj(x, g1, wlat, why, wna, gq, wq, wqs, gkv, wk, wvt, cq, sq, ck, sk, *, tm):
    B, S, D = x.shape
    ns = S // tm
    hyw = why.shape[1]
    qw = MLA_HEADS * HEAD_PAD
    tok = lambda w: pl.BlockSpec((1, tm, w), lambda i: (i // ns, i % ns, 0))
    tab = pl.BlockSpec((tm, HEAD_PAD), lambda i: (i % ns, 0))
    in_specs = [tok(D), _const_spec(g1.shape), _const_spec(wlat.shape), _const_spec(why.shape),
                _const_spec(wna.shape), _const_spec(gq.shape), _const_spec(wq.shape),
                _const_spec(wqs.shape), _const_spec(gkv.shape), _const_spec(wk.shape),
                _const_spec(wvt.shape), tab, tab, tab, tab]
    out_specs = [tok(qw), tok(qw),
                 pl.BlockSpec((1, MLA_WIDTH, tm), lambda i: (i // ns, 0, i % ns)),
                 pl.BlockSpec((tm, hyw), lambda i: (i % ns, i // ns)),
                 tok(NA_WIDTH), tok(NA_WIDTH), tok(NA_WIDTH)]
    out_shape = [jax.ShapeDtypeStruct((B, S, qw), BF16),
                 jax.ShapeDtypeStruct((B, S, qw), BF16),
                 jax.ShapeDtypeStruct((B, MLA_WIDTH, S), BF16),
                 jax.ShapeDtypeStruct((S, B * hyw), F32),
                 jax.ShapeDtypeStruct((B, S, NA_WIDTH), BF16),
                 jax.ShapeDtypeStruct((B, S, NA_WIDTH), BF16),
                 jax.ShapeDtypeStruct((B, S, NA_WIDTH), BF16)]
    return pl.pallas_call(
        _inproj_kernel, grid=(B * ns,), in_specs=in_specs, out_specs=out_specs,
        out_shape=out_shape, compiler_params=_params(("parallel",)), name="inproj",
    )(x, g1, wlat, why, wna, gq, wq, wqs, gkv, wk, wvt, cq, sq, ck, sk)


def _mla_attn_kernel(q_ref, k_ref, vt_ref, o_ref, *, tk):
    S = k_ref.shape[1]
    tq = q_ref.shape[1]
    nk = S // tk
    outs = []
    for hd in range(MLA_HEADS):
        sl = slice(hd * HEAD_PAD, (hd + 1) * HEAD_PAD)
        vs = slice(hd * MLA_V, (hd + 1) * MLA_V)
        qh = q_ref[0, :, sl]

        def body(j, carry, sl=sl, vs=vs, qh=qh):
            m, l, acc = carry
            off = pl.multiple_of(j * tk, tk)
            kj = k_ref[0, pl.ds(off, tk), sl]
            s = lax.dot_general(kj, qh, (((1,), (1,)), ((), ())),
                                preferred_element_type=F32)
            m_new = jnp.maximum(m, jnp.max(s, axis=0, keepdims=True))
            alpha = jnp.exp(m - m_new)
            p = jnp.exp(s - m_new)
            l = alpha * l + jnp.sum(p, axis=0, keepdims=True)
            vj = vt_ref[0, vs, pl.ds(off, tk)]
            acc = alpha * acc + jnp.dot(vj, p.astype(BF16), preferred_element_type=F32)
            return m_new, l, acc

        init = (jnp.full((1, tq), -jnp.inf, F32), jnp.zeros((1, tq), F32),
                jnp.zeros((MLA_V, tq), F32))
        m, l, acc = lax.fori_loop(0, nk, body, init)
        outs.append(acc / l)
    o_ref[0] = jnp.concatenate(outs, axis=0).T


def _mla_attn(q, k, vt, *, tq, tk):
    B, S, qw = q.shape
    return pl.pallas_call(
        functools.partial(_mla_attn_kernel, tk=tk),
        grid=(B, S // tq),
        in_specs=[pl.BlockSpec((1, tq, qw), lambda b, i: (b, i, 0)),
                  pl.BlockSpec((1, S, qw), lambda b, i: (b, 0, 0)),
                  pl.BlockSpec((1, MLA_WIDTH, S), lambda b, i: (b, 0, 0))],
        out_specs=pl.BlockSpec((1, tq, MLA_WIDTH), lambda b, i: (b, i, 0)),
        out_shape=jax.ShapeDtypeStruct((B, S, MLA_WIDTH), F32),
        compiler_params=_params(("parallel", "arbitrary")), name="mla_attn",
    )(q, k, vt)


def _natten_kernel(q_ref, k_ref, v_ref, bias_ref, o_ref, *, R, KH):
    r = pl.program_id(1)
    start = jnp.clip(r - KH // 2, 0, R - KH)
    dr0 = start - r + (NA_KH - 1)
    off = pl.multiple_of(start * GRID_W, GRID_W)
    kw = k_ref[0, pl.ds(off, KH * GRID_W), :]
    vw = v_ref[0, pl.ds(off, KH * GRID_W), :]
    q = q_ref[0]
    col_head = lax.broadcasted_iota(jnp.int32, (1, NA_WIDTH), 1) // NA_HEAD_DIM
    y = jnp.zeros((GRID_W, NA_WIDTH), F32)
    for hd in range(NA_HEADS):
        sel = col_head == hd
        qh = jnp.where(sel, q, jnp.zeros_like(q))
        s = lax.dot_general(qh, kw, (((1,), (1,)), ((), ())),
                            preferred_element_type=F32)
        bias = jnp.concatenate(
            [bias_ref[hd, dr0 + kh] for kh in range(0, KH, 2)], axis=-1)
        logits = s + bias
        m = jnp.max(logits, axis=-1, keepdims=True)
        p = jnp.exp(logits - m)
        l = jnp.sum(p, axis=-1, keepdims=True)
        o = jnp.dot(p.astype(BF16), vw, preferred_element_type=F32)
        y = y + jnp.where(sel, o / l, 0.0)
    o_ref[0] = y


def _natten(q, k, v, bias_pairs):
    B, S, _ = q.shape
    R = S // GRID_W
    KH = min(NA_KH, R)
    assert KH % 2 == 0
    return pl.pallas_call(
        functools.partial(_natten_kernel, R=R, KH=KH),
        grid=(B, R),
        in_specs=[pl.BlockSpec((1, GRID_W, NA_WIDTH), lambda b, r: (b, r, 0)),
                  pl.BlockSpec((1, S, NA_WIDTH), lambda b, r: (b, 0, 0)),
                  pl.BlockSpec((1, S, NA_WIDTH), lambda b, r: (b, 0, 0)),
                  _const_spec(bias_pairs.shape)],
        out_specs=pl.BlockSpec((1, GRID_W, NA_WIDTH), lambda b, r: (b, r, 0)),
        out_shape=jax.ShapeDtypeStruct((B, S, NA_WIDTH), F32),
        compiler_params=_params(("parallel", "arbitrary")), name="natten",
    )(q, k, v, bias_pairs)


def _natten_bias_pairs(rpb):
    c = jnp.arange(GRID_W)
    start = jnp.clip(c - NA_KW // 2, 0, GRID_W - NA_KW)
    v = c[None, :]
    inwin = (v >= start[:, None]) & (v < start[:, None] + NA_KW)
    dc = jnp.clip(v - c[:, None] + (NA_KW - 1), 0, 2 * NA_KW - 2)
    t = jnp.where(inwin[None, None], rpb[:, :, dc], MASK_VALUE)
    return jnp.concatenate([t[:, :-1], t[:, 1:]], axis=-1).astype(F32)


def _short_conv_kernel(v_ref, x1_ref, x2_ref, wv_ref, w1_ref, w2_ref, bv_ref, b1_ref, b2_ref,
                       u_ref, g1_ref, g2_ref):
    L = v_ref.shape[0]
    t = lax.broadcasted_iota(jnp.int32, v_ref.shape, 0)

    def conv(x_ref, w_ref, b_ref):
        x = x_ref[...]
        prev = jnp.where(t == 0, 0.0, pltpu.roll(x, 1, axis=0))
        nxt = jnp.where(t == L - 1, 0.0, pltpu.roll(x, L - 1, axis=0))
        w = w_ref[...]
        return w[0:1] * prev + w[1:2] * x + w[2:3] * nxt + b_ref[...]

    u_ref[...] = conv(v_ref, wv_ref, bv_ref).astype(BF16)
    g1_ref[...] = conv(x1_ref, w1_ref, b1_ref)
    g2_ref[...] = conv(x2_ref, w2_ref, b2_ref)


def _short_conv(hy, conv_w, conv_b, B):
    L = hy.shape[0]
    C = HY_WIDTH
    nj = C // LANE
    per_b = 3 * nj
    seg = lambda s: pl.BlockSpec((L, LANE), lambda b, j: (0, b * per_b + s * nj + j))
    wseg = lambda s: pl.BlockSpec((3, LANE), lambda b, j: (0, s * nj + j))
    bseg = lambda s: pl.BlockSpec((1, LANE), lambda b, j: (0, s * nj + j))
    out = pl.BlockSpec((L, LANE), lambda b, j: (0, b * nj + j))
    return pl.pallas_call(
        _short_conv_kernel, grid=(B, nj),
        in_specs=[seg(0), seg(1), seg(2), wseg(0), wseg(1), wseg(2), bseg(0), bseg(1), bseg(2)],
        out_specs=[out, out, out],
        out_shape=[jax.ShapeDtypeStruct((L, B * C), BF16),
                   jax.ShapeDtypeStruct((L, B * C), F32),
                   jax.ShapeDtypeStruct((L, B * C), F32)],
        compiler_params=_params(("parallel", "parallel")), name="hy_short_conv",
    )(hy, hy, hy, conv_w, conv_w, conv_w, conv_b, conv_b, conv_b)


def _filter_kernel(z_ref, w1_ref, b1_ref, f1_ref, w2_ref, b2_ref, f2_ref, w3_ref, dec_ref,
                   hp_ref, hm_ref):
    hi = lax.Precision.HIGHEST
    dot = lambda a, b: jnp.dot(a, b, precision=hi, preferred_element_type=F32)
    h = jnp.sin(f1_ref[...] * (dot(z_ref[...], w1_ref[...]) + b1_ref[...]))
    h = jnp.sin(f2_ref[...] * (dot(h, w2_ref[...]) + b2_ref[...]))
    h = dot(h, w3_ref[...])
    dec = dec_ref[...]
    C = HY_WIDTH
    for o in range(HY_ORDER):
        hf = h[:, (2 * o) * C:(2 * o + 1) * C] * dec
        hb = h[:, (2 * o + 1) * C:(2 * o + 2) * C] * dec
        hp_ref[:, o * C:(o + 1) * C] = (hf + hb).astype(BF16)
        hm_ref[:, o * C:(o + 1) * C] = (hf - hb).astype(BF16)


def _filters(z, w1, b1, f1, w2, b2, f2, w3, decay, *, tm):
    L = z.shape[0]
    OC = HY_ORDER * HY_WIDTH
    row = lambda w: pl.BlockSpec((tm, w), lambda i: (i, 0))
    return pl.pallas_call(
        _filter_kernel, grid=(L // tm,),
        in_specs=[row(z.shape[1]), _const_spec(w1.shape), _const_spec(b1.shape),
                  _const_spec(f1.shape), _const_spec(w2.shape), _const_spec(b2.shape),
                  _const_spec(f2.shape), _const_spec(w3.shape), row(HY_WIDTH)],
        out_specs=[row(OC), row(OC)],
        out_shape=[jax.ShapeDtypeStruct((L, OC), BF16), jax.ShapeDtypeStruct((L, OC), BF16)],
        compiler_params=_params(("parallel",)), name="hy_filters",
    )(z, w1, b1, f1, w2, b2, f2, w3, decay)


def _coef_kernel(c_ref, sp_ref, sp0_ref, hp_ref, hm_ref, skip_ref, a_ref, b_ref, cc_ref, *, L):
    i = pl.program_id(0)
    tk = c_ref.shape[0]
    inv = 1.0 / L
    hp = hp_ref[...]
    skip = skip_ref[...]
    gr = jnp.dot(c_ref[...], hp, preferred_element_type=F32) + skip
    qm = jnp.dot(sp_ref[...], hm_ref[...], preferred_element_type=F32)
    nyq = jnp.dot(sp0_ref[...], hp, preferred_element_type=F32)[0:1] + skip
    is0 = (i * tk + lax.broadcasted_iota(jnp.int32, gr.shape, 0)) == 0
    a = jnp.where(is0, 0.5 * inv, inv) * gr
    a_ref[...] = a
    b_ref[...] = jnp.where(is0, 0.0, -inv * qm)
    cc_ref[...] = jnp.where(is0, (0.5 * inv) * nyq, a)


def _coefs(cmat, spmat, hp, hm, skip, *, tk):
    L = cmat.shape[0]
    OC = hp.shape[1]
    blk = pl.BlockSpec((tk, L), lambda i: (i, 0))
    out = pl.BlockSpec((tk, OC), lambda i: (i, 0))
    shp = jax.ShapeDtypeStruct((L, OC), F32)
    return pl.pallas_call(
        functools.partial(_coef_kernel, L=L), grid=(L // tk,),
        in_specs=[blk, blk, pl.BlockSpec((16, L), lambda i: (0, 0)),
                  _const_spec(hp.shape), _const_spec(hm.shape), _const_spec(skip.shape)],
        out_specs=[out, out, out], out_shape=[shp, shp, shp],
        compiler_params=_params(("parallel",)), name="hy_coefs",
    )(cmat, spmat, spmat, hp, hm, skip)


def _fwd_dft_kernel(c_ref, sp_ref, u_ref, a_ref, b_ref, cc_ref, r1_ref, r2_ref, *, nb):
    u = u_ref[...]
    p = jnp.dot(c_ref[...], u, preferred_element_type=F32)
    q = jnp.dot(sp_ref[...], u, preferred_element_type=F32)
    tile = lambda x: jnp.concatenate([x] * nb, axis=-1) if nb > 1 else x
    a = tile(a_ref[...])
    b = tile(b_ref[...])
    cc = tile(cc_ref[...])
    r1_ref[...] = (p * a + q * b).astype(BF16)
    r2_ref[...] = (q * cc - p * b).astype(BF16)


def _fwd_dft(cmat, spmat, u, a, b, cc, order, *, tk, nb):
    L = cmat.shape[0]
    C = HY_WIDTH
    cb = nb * C
    ncb = u.shape[1] // cb
    blk = pl.BlockSpec((tk, L), lambda c, j: (j, 0))
    coef = pl.BlockSpec((tk, C), lambda c, j: (j, order))
    out = pl.BlockSpec((tk, cb), lambda c, j: (j, c))
    shp = jax.ShapeDtypeStruct(u.shape, BF16)
    return pl.pallas_call(
        functools.partial(_fwd_dft_kernel, nb=nb), grid=(ncb, L // tk),
        in_specs=[blk, blk, pl.BlockSpec((L, cb), lambda c, j: (0, c)), coef, coef, coef],
        out_specs=[out, out], out_shape=[shp, shp],
        compiler_params=_params(("parallel", "arbitrary")), name="hy_fwd_dft",
    )(cmat, spmat, u, a, b, cc)


def _inv_dft_kernel(c_ref, st_ref, r1_ref, r2_ref, g_ref, z_ref):
    y = (jnp.dot(c_ref[...], r1_ref[...], preferred_element_type=F32)
         + jnp.dot(st_ref[...], r2_ref[...], preferred_element_type=F32))
    z_ref[...] = (g_ref[...] * y).astype(z_ref.dtype)


def _inv_dft(cmat, stmat, r1, r2, gate, out_dtype, *, tm, nb):
    L = cmat.shape[0]
    cb = nb * HY_WIDTH
    ncb = r1.shape[1] // cb
    blk = pl.BlockSpec((tm, L), lambda c, i: (i, 0))
    full = pl.BlockSpec((L, cb), lambda c, i: (0, c))
    tile = pl.BlockSpec((tm, cb), lambda c, i: (i, c))
    return pl.pallas_call(
        _inv_dft_kernel, grid=(ncb, L // tm),
        in_specs=[blk, blk, full, full, tile],
        out_specs=tile, out_shape=jax.ShapeDtypeStruct(r1.shape, out_dtype),
        compiler_params=_params(("parallel", "arbitrary")), name="hy_inv_dft",
    )(cmat, stmat, r1, r2, gate)


def _hyena_features(L):
    t_idx = jnp.arange(L, dtype=F32)[:, None]
    t_norm = jnp.linspace(0.0, 1.0, L, dtype=F32)[:, None]
    bands = jnp.linspace(1e-4, HY_BANDS - 1, HY_BANDS, dtype=F32)[None, :]
    ang = 2.0 * math.pi * t_idx * bands / L
    z = jnp.concatenate([t_norm, jnp.cos(ang), jnp.sin(ang)], axis=-1)
    z = jnp.pad(z, ((0, 0), (0, LANE - HY_EMB)))
    deltas = jnp.linspace(math.log(HY_TARGET) / HY_SLOW_DECAY,
                          math.log(HY_TARGET) / HY_FAST_DECAY, HY_WIDTH, dtype=F32)
    decay = jnp.exp(-t_norm * jnp.abs(deltas)[None, :])
    return z, decay


def _mix_ffn_kernel(x_ref, ya_ref, yb_ref, yc_ref, gmix_ref, wout_ref, g2_ref, wg_ref, wu_ref,
                    wd_ref, gf_ref, o_ref, *, n_chunks, final_norm):
    gmix = gmix_ref[...]
    ca = ya_ref.shape[-1]
    cb = yb_ref.shape[-1]
    ymix = jnp.concatenate([
        _rms(ya_ref[0], gmix[:, :ca]),
        _rms(yb_ref[...], gmix[:, ca:ca + cb]),
        _rms(yc_ref[0], gmix[:, ca + cb:]),
    ], axis=-1).astype(BF16)
    x = x_ref[0] + jnp.dot(ymix, wout_ref[...], preferred_element_type=F32)
    h2 = _rms(x, g2_ref[...]).astype(BF16)
    ff = wg_ref.shape[1]
    ch = ff // n_chunks
    acc = x
    for c in range(n_chunks):
        sl = slice(c * ch, (c + 1) * ch)
        gate = jnp.dot(h2, wg_ref[:, sl], preferred_element_type=F32)
        up = jnp.dot(h2, wu_ref[:, sl], preferred_element_type=F32)
        act = (gate * jax.nn.sigmoid(gate) * up).astype(BF16)
        acc = acc + jnp.dot(act, wd_ref[sl, :], preferred_element_type=F32)
    if final_norm:
        acc = _rms(acc, gf_ref[...])
    o_ref[0] = acc


def _mix_ffn(x, ya, yb, yc, gmix, wout, g2, wg, wu, wd, gf, *, tm, final_norm):
    B, S, D = x.shape
    ns = S // tm
    tok = lambda w: pl.BlockSpec((1, tm, w), lambda i: (i // ns, i % ns, 0))
    once = lambda a: pl.BlockSpec(a.shape, lambda i: (0,) * a.ndim,
                                  pipeline_mode=pl.Buffered(1))
    n_chunks = 2 if (wg.shape[1] // 2) % LANE == 0 else 1
    return pl.pallas_call(
        functools.partial(_mix_ffn_kernel, n_chunks=n_chunks, final_norm=final_norm),
        grid=(B * ns,),
        in_specs=[tok(D), tok(ya.shape[-1]),
                  pl.BlockSpec((tm, HY_WIDTH), lambda i: (i % ns, i // ns)),
                  tok(yc.shape[-1]), once(gmix), once(wout), once(g2), once(wg), once(wu),
                  once(wd), once(gf)],
        out_specs=tok(D), out_shape=jax.ShapeDtypeStruct((B, S, D), F32),
        compiler_params=_params(("parallel",)), name="mix_ffn",
    )(x, ya, yb, yc, gmix, wout, g2, wg, wu, wd, gf)


def _rope_tables(S):
    half = MLA_ROPE // 2
    pos = jnp.arange(S, dtype=F32)
    inv = ROPE_THETA ** (-jnp.arange(0, MLA_ROPE, 2, dtype=F32) / MLA_ROPE)
    ang = pos[:, None] * inv[None, :]
    cos, sin = jnp.cos(ang), jnp.sin(ang)
    z_lo = jnp.zeros((S, MLA_NOPE), F32)
    z_hi = jnp.zeros((S, HEAD_PAD - MLA_NOPE - 2 * half), F32)
    ck = jnp.concatenate([z_lo, cos, cos, z_hi], axis=-1)
    sk = jnp.concatenate([z_lo, -sin, sin, z_hi], axis=-1)
    scale = 1.0 / math.sqrt(MLA_NOPE + MLA_ROPE)
    cq = jnp.concatenate([jnp.ones((S, MLA_NOPE), F32), cos, cos, z_hi], axis=-1) * scale
    sq = sk * scale
    return cq, sq, ck, sk


def _pad_heads(w, n_heads, width, keep):
    K = w.shape[0]
    w = w.reshape(K, n_heads, width)[:, :, :keep]
    w = jnp.pad(w, ((0, 0), (0, 0), (0, HEAD_PAD - keep)))
    return w.reshape(K, n_heads * HEAD_PAD)


def _swap_rope_cols(w, n_heads, width):
    K = w.shape[0]
    half = MLA_ROPE // 2
    w = w.reshape(K, n_heads, width)
    a = w[:, :, MLA_NOPE:MLA_NOPE + half]
    b = w[:, :, MLA_NOPE + half:MLA_NOPE + 2 * half]
    out = jnp.concatenate([jnp.zeros_like(w[:, :, :MLA_NOPE]), b, a], axis=-1)
    return out.reshape(K, n_heads * width)


def _layer_weights(w_in, w_uq, w_ukv):
    D = w_in.shape[0]
    o = MLA_Q_RANK + MLA_KV_RANK
    w_kpe = w_in[:, o:o + MLA_ROPE]
    zl = jnp.zeros((D, MLA_NOPE), F32)
    zh = jnp.zeros((D, HEAD_PAD - MLA_NOPE - MLA_ROPE), F32)
    half = MLA_ROPE // 2
    kpe_pad = jnp.concatenate([zl, w_kpe, zh], axis=-1)
    kpe_sw = jnp.concatenate([zl, w_kpe[:, half:], w_kpe[:, :half], zh], axis=-1)
    wlat = jnp.concatenate([w_in[:, :o], kpe_pad, kpe_sw], axis=-1).astype(BF16)
    o2 = o + MLA_ROPE
    why = w_in[:, o2:o2 + 3 * HY_WIDTH].astype(BF16)
    wna = w_in[:, o2 + 3 * HY_WIDTH:].astype(BF16)
    qd = MLA_NOPE + MLA_ROPE
    wq = _pad_heads(w_uq, MLA_HEADS, qd, qd).astype(BF16)
    wqs = _pad_heads(_swap_rope_cols(w_uq, MLA_HEADS, qd), MLA_HEADS, qd, qd).astype(BF16)
    kvd = MLA_NOPE + MLA_V
    wk = _pad_heads(w_ukv, MLA_HEADS, kvd, MLA_NOPE).astype(BF16)
    wv = w_ukv.reshape(-1, MLA_HEADS, kvd)[:, :, MLA_NOPE:].reshape(-1, MLA_WIDTH)
    wvt = wv.T.astype(BF16)
    return wlat, why, wna, wq, wqs, wk, wvt


def _pad2(a, rows, cols):
    return jnp.pad(a, ((0, rows - a.shape[0]), (0, cols - a.shape[1])))


def kernel(x, norm1_g, w_in, mla_q_norm_g, mla_w_uq, mla_kv_norm_g, mla_w_ukv, hy_conv_w, hy_conv_b, hy_filt_w1, hy_filt_b1, hy_filt_freq1, hy_filt_w2, hy_filt_b2, hy_filt_freq2, hy_filt_w3, hy_skip, na_rpb, mix_norm_g, w_out, norm2_g, ffn_w_gate, ffn_w_up, ffn_w_down, final_norm_g):
    B, S, D = x.shape
    depth = w_in.shape[0]
    L = S
    tm = min(512, S)
    tq = min(512, S)
    tkv = min(512, S)
    t_dft = min(256, L)
    nb = 2 if B % 2 == 0 else 1

    cq, sq, ck, sk = _rope_tables(S)
    cmat, spmat, stmat = _dft_tables(L)
    z_feat, decay = _hyena_features(L)
    row = lambda v: v.reshape(1, -1)

    for l in range(depth):
        wlat, why, wna, wq, wqs, wk, wvt = _layer_weights(w_in[l], mla_w_uq[l], mla_w_ukv[l])
        q, k, vt, hy, naq, nak, nav = _inproj(
            x, row(norm1_g[l]), wlat, why, wna, row(mla_q_norm_g[l]), wq, wqs,
            row(mla_kv_norm_g[l]), wk, wvt, cq, sq, ck, sk, tm=tm)

        y_a = _mla_attn(q, k, vt, tq=tq, tk=tkv)
        y_c = _natten(naq, nak, nav, _natten_bias_pairs(na_rpb[l]))

        u, g1, g2 = _short_conv(hy, hy_conv_w[l], row(hy_conv_b[l]), B)
        hp, hm = _filters(
            z_feat, _pad2(hy_filt_w1[l], LANE, LANE), _pad2(row(hy_filt_b1[l]), 1, LANE),
            _pad2(row(hy_filt_freq1[l]), 1, LANE), _pad2(hy_filt_w2[l], LANE, LANE),
            _pad2(row(hy_filt_b2[l]), 1, LANE), _pad2(row(hy_filt_freq2[l]), 1, LANE),
            _pad2(hy_filt_w3[l], LANE, hy_filt_w3.shape[-1]), decay, tm=tm)
        a, b, cc = _coefs(cmat, spmat, hp, hm, row(hy_skip[l]), tk=t_dft)
        r1, r2 = _fwd_dft(cmat, spmat, u, a, b, cc, 0, tk=t_dft, nb=nb)
        z1 = _inv_dft(cmat, stmat, r1, r2, g1, BF16, tm=t_dft, nb=nb)
        r1, r2 = _fwd_dft(cmat, spmat, z1, a, b, cc, 1, tk=t_dft, nb=nb)
        y_b = _inv_dft(cmat, stmat, r1, r2, g2, F32, tm=t_dft, nb=nb)

        x = _mix_ffn(x, y_a, y_b, y_c, row(mix_norm_g[l]), w_out[l].astype(BF16),
                     row(norm2_g[l]), ffn_w_gate[l].astype(BF16), ffn_w_up[l].astype(BF16),
                     ffn_w_down[l].astype(BF16), row(final_norm_g),
                     tm=tm, final_norm=(l == depth - 1))
    return x
```

```python
import functools
import math

import jax
import jax.numpy as jnp
from jax import lax
from jax.experimental import pallas as pl
from jax.experimental.pallas import tpu as pltpu

F32 = jnp.float32
BF16 = jnp.bfloat16

NORM_EPS = 1e-6
MLA_HEADS = 6
MLA_NOPE = 64
MLA_ROPE = 32
MLA_V = 64
MLA_Q_RANK = 256
MLA_KV_RANK = 128
ROPE_THETA = 10000.0
HY_WIDTH = 384
HY_ORDER = 2
HY_BANDS = 8
HY_EMB = 1 + 2 * HY_BANDS
HY_FFN = 64
HY_FAST_DECAY = 0.3
HY_SLOW_DECAY = 1.5
HY_TARGET = 1e-2
NA_HEADS = 4
NA_HEAD_DIM = 64
NA_WIDTH = NA_HEADS * NA_HEAD_DIM
GRID_W = 64
NA_KH = 8
NA_KW = 16
MLA_WIDTH = MLA_HEADS * MLA_V

LANE = 128
HEAD_PAD = 128
BF16_SUBLANES = 16
MLA_VPAD = MLA_V + BF16_SUBLANES
VMEM_LIMIT = 56 * 1024 * 1024
MASK_VALUE = -1e30


def _params(sem, vmem=VMEM_LIMIT):
    return pltpu.CompilerParams(dimension_semantics=sem, vmem_limit_bytes=vmem)


def _rms(x, g):
    return x * lax.rsqrt(jnp.mean(x * x, axis=-1, keepdims=True) + NORM_EPS) * g


def _const_spec(shape):
    nd = len(shape)
    return pl.BlockSpec(shape, lambda *_: (0,) * nd)


def _dft_table_kernel(c_ref, sp_ref, st_ref, ec_ref, es_ref, *, L, tk):
    i = pl.program_id(0)
    n = 2 * L
    w = 2.0 * math.pi / n

    @pl.when(i == 0)
    def _():
        r = lax.broadcasted_iota(jnp.int32, (tk, L), 0)
        t = lax.broadcasted_iota(jnp.int32, (tk, L), 1)
        ang = ((r * t) & (n - 1)).astype(F32) * w
        ec_ref[...] = jnp.cos(ang)
        es_ref[...] = jnp.sin(ang)

    k0 = i * tk
    t1 = lax.broadcasted_iota(jnp.int32, (1, L), 1)
    ang0 = ((k0 * t1) & (n - 1)).astype(F32) * w
    ca = jnp.cos(ang0)
    sa = jnp.sin(ang0)
    ec = ec_ref[...]
    es = es_ref[...]
    cv = ca * ec - sa * es
    sv = sa * ec + ca * es
    rows = k0 + lax.broadcasted_iota(jnp.int32, (tk, L), 0)
    cols = lax.broadcasted_iota(jnp.int32, (tk, L), 1)
    sgn_c = (1 - 2 * (cols & 1)).astype(F32)
    sgn_r = (1 - 2 * (rows & 1)).astype(F32)
    c_ref[...] = cv.astype(BF16)
    sp_ref[...] = jnp.where(rows == 0, sgn_c, sv).astype(BF16)
    st_ref[...] = jnp.where(cols == 0, sgn_r, sv).astype(BF16)


def _dft_tables(L):
    assert L & (L - 1) == 0, "sequence length must be a power of two"
    tk = min(128, L)
    spec = pl.BlockSpec((tk, L), lambda i: (i, 0))
    shp = jax.ShapeDtypeStruct((L, L), BF16)
    return pl.pallas_call(
        functools.partial(_dft_table_kernel, L=L, tk=tk),
        grid=(L // tk,),
        out_specs=[spec, spec, spec],
        out_shape=[shp, shp, shp],
        scratch_shapes=[pltpu.VMEM((tk, L), F32), pltpu.VMEM((tk, L), F32)],
        compiler_params=_params(("arbitrary",)),
        name="dft_tables",
    )()


def _inproj_kernel(x_ref, g1_ref, wlat_ref, why_ref, wna_ref, gq_ref, wq_ref, wqs_ref,
                   gkv_ref, wk_ref, wvt_ref, cq_ref, sq_ref, ck_ref, sk_ref,
                   q_ref, k_ref, vt_ref, hy_ref, naq_ref, nak_ref, nav_ref):
    x = x_ref[0]
    h = _rms(x, g1_ref[...]).astype(BF16)
    lat = jnp.dot(h, wlat_ref[...], preferred_element_type=F32)
    hy_ref[...] = jnp.dot(h, why_ref[...], preferred_element_type=F32)
    na = jnp.dot(h, wna_ref[...], preferred_element_type=F32)
    naq_ref[0] = (na[:, :NA_WIDTH] * (1.0 / math.sqrt(NA_HEAD_DIM))).astype(BF16)
    nak_ref[0] = na[:, NA_WIDTH:2 * NA_WIDTH].astype(BF16)
    nav_ref[0] = na[:, 2 * NA_WIDTH:].astype(BF16)

    c_q = lat[:, :MLA_Q_RANK]
    c_kv = lat[:, MLA_Q_RANK:MLA_Q_RANK + MLA_KV_RANK]
    o = MLA_Q_RANK + MLA_KV_RANK
    kpe = lat[:, o:o + HEAD_PAD]
    kpe_sw = lat[:, o + HEAD_PAD:o + 2 * HEAD_PAD]

    cqn = _rms(c_q, gq_ref[...]).astype(BF16)
    qf = jnp.dot(cqn, wq_ref[...], preferred_element_type=F32)
    qs = jnp.dot(cqn, wqs_ref[...], preferred_element_type=F32)
    cq = cq_ref[...]
    sq = sq_ref[...]
    ckvn = _rms(c_kv, gkv_ref[...])
    kf = jnp.dot(ckvn.astype(BF16), wk_ref[...], preferred_element_type=F32)
    kpe_r = kpe * ck_ref[...] + kpe_sw * sk_ref[...]
    for hd in range(MLA_HEADS):
        sl = slice(hd * HEAD_PAD, (hd + 1) * HEAD_PAD)
        q_ref[0, :, sl] = (qf[:, sl] * cq + qs[:, sl] * sq).astype(BF16)
        k_ref[0, :, sl] = (kf[:, sl] + kpe_r).astype(BF16)
    vt = jnp.dot(wvt_ref[...], ckvn.T.astype(BF16), preferred_element_type=F32).astype(BF16)
    ones = jnp.ones((MLA_VPAD - MLA_V, vt.shape[1]), BF16)
    for hd in range(MLA_HEADS):
        vt_ref[0, hd * MLA_VPAD:hd * MLA_VPAD + MLA_V, :] = vt[hd * MLA_V:(hd + 1) * MLA_V]
        vt_ref[0, hd * MLA_VPAD + MLA_V:(hd + 1) * MLA_VPAD, :] = ones


def _inproj(x, g1, wlat, why, wna, gq, wq, wqs, gkv, wk, wvt, cq, sq, ck, sk, *, tm):
    B, S, D = x.shape
    ns = S // tm
    hyw = why.shape[1]
    qw = MLA_HEADS * HEAD_PAD
    tok = lambda w: pl.BlockSpec((1, tm, w), lambda i: (i // ns, i % ns, 0))
    tab = pl.BlockSpec((tm, HEAD_PAD), lambda i: (i % ns, 0))
    in_specs = [tok(D), _const_spec(g1.shape), _const_spec(wlat.shape), _const_spec(why.shape),
                _const_spec(wna.shape), _const_spec(gq.shape), _const_spec(wq.shape),
                _const_spec(wqs.shape), _const_spec(gkv.shape), _const_spec(wk.shape),
                _const_spec(wvt.shape), tab, tab, tab, tab]
    out_specs = [tok(qw), tok(qw),
                 pl.BlockSpec((1, MLA_HEADS * MLA_VPAD, tm), lambda i: (i // ns, 0, i % ns)),
                 pl.BlockSpec((tm, hyw), lambda i: (i % ns, i // ns)),
                 tok(NA_WIDTH), tok(NA_WIDTH), tok(NA_WIDTH)]
    out_shape = [jax.ShapeDtypeStruct((B, S, qw), BF16),
                 jax.ShapeDtypeStruct((B, S, qw), BF16),
                 jax.ShapeDtypeStruct((B, MLA_HEADS * MLA_VPAD, S), BF16),
                 jax.ShapeDtypeStruct((S, B * hyw), F32),
                 jax.ShapeDtypeStruct((B, S, NA_WIDTH), BF16),
                 jax.ShapeDtypeStruct((B, S, NA_WIDTH), BF16),
                 jax.ShapeDtypeStruct((B, S, NA_WIDTH), BF16)]
    return pl.pallas_call(
        _inproj_kernel, grid=(B * ns,), in_specs=in_specs, out_specs=out_specs,
        out_shape=out_shape, compiler_params=_params(("parallel",)), name="inproj",
    )(x, g1, wlat, why, wna, gq, wq, wqs, gkv, wk, wvt, cq, sq, ck, sk)


def _mla_attn_kernel(q_ref, k_ref, vt_ref, o_ref, m_ref, acc_ref, *, tk, unroll):
    S = k_ref.shape[1]
    nk = S // tk
    m_ref[...] = jnp.full(m_ref.shape, -jnp.inf, F32)
    acc_ref[...] = jnp.zeros(acc_ref.shape, F32)

    def body(j, carry):
        base = j * (unroll * tk)

        def scores(c, hd):
            off = pl.multiple_of(base + c * tk, tk)
            sl = slice(hd * HEAD_PAD, (hd + 1) * HEAD_PAD)
            kj = k_ref[0, pl.ds(off, tk), sl]
            s = lax.dot_general(kj, q_ref[0, :, sl], (((1,), (1,)), ((), ())),
                                preferred_element_type=F32)
            m_old = m_ref[hd]
            m_new = jnp.maximum(m_old, jnp.max(s, axis=0, keepdims=True))
            m_ref[hd] = m_new
            return s, m_old, m_new

        def probs(s, m_old, m_new):
            return jnp.exp2(s - m_new).astype(BF16), jnp.exp2(m_old - m_new)

        def accumulate(c, hd, p, alpha):
            off = pl.multiple_of(base + c * tk, tk)
            vj = vt_ref[0, hd * MLA_VPAD:(hd + 1) * MLA_VPAD, pl.ds(off, tk)]
            acc_ref[hd] = alpha * acc_ref[hd] + jnp.dot(vj, p, preferred_element_type=F32)

        items = [(c, hd) for c in range(unroll) for hd in range(MLA_HEADS)]
        n = len(items)
        st_scores, st_probs = {}, {}
        for t in range(n + 2):
            if t < n:
                st_scores[t] = scores(*items[t])
            if 0 <= t - 1 < n:
                st_probs[t - 1] = probs(*st_scores.pop(t - 1))
            if 0 <= t - 2 < n:
                accumulate(*items[t - 2], *st_probs.pop(t - 2))
        return carry

    lax.fori_loop(0, nk // unroll, body, 0)
    outs = []
    for hd in range(MLA_HEADS):
        acc = acc_ref[hd]
        outs.append(acc[:MLA_V] / acc[MLA_V:MLA_V + 1])
    o_ref[0] = jnp.concatenate(outs, axis=0).T


def _mla_attn(q, k, vt, *, tq, tk, unroll):
    B, S, qw = q.shape
    vrows = vt.shape[1]
    assert S % (tk * unroll) == 0
    return pl.pallas_call(
        functools.partial(_mla_attn_kernel, tk=tk, unroll=unroll),
        grid=(B, S // tq),
        in_specs=[pl.BlockSpec((1, tq, qw), lambda b, i: (b, i, 0)),
                  pl.BlockSpec((1, S, qw), lambda b, i: (b, 0, 0)),
                  pl.BlockSpec((1, vrows, S), lambda b, i: (b, 0, 0))],
        out_specs=pl.BlockSpec((1, tq, MLA_WIDTH), lambda b, i: (b, i, 0)),
        out_shape=jax.ShapeDtypeStruct((B, S, MLA_WIDTH), F32),
        scratch_shapes=[pltpu.VMEM((MLA_HEADS, 1, tq), F32),
                        pltpu.VMEM((MLA_HEADS, MLA_VPAD, tq), F32)],
        compiler_params=_params(("parallel", "arbitrary")), name="mla_attn",
    )(q, k, vt)


def _natten_kernel(q_ref, k_ref, v_ref, bias_ref, o_ref, *, R, KH, G):
    col_head = lax.broadcasted_iota(jnp.int32, (1, NA_WIDTH), 1) // NA_HEAD_DIM
    sels = [col_head == hd for hd in range(NA_HEADS)]

    def window(i):
        r = pl.program_id(1) * G + i
        start = jnp.clip(r - KH // 2, 0, R - KH)
        return start - r + (NA_KH - 1), pl.multiple_of(start * GRID_W, GRID_W)

    def scores(i):
        dr0, off = window(i)
        q = q_ref[0, i * GRID_W:(i + 1) * GRID_W, :]
        qm = jnp.concatenate([jnp.where(sel, q, jnp.zeros_like(q)) for sel in sels], axis=0)
        kw = k_ref[0, pl.ds(off, KH * GRID_W), :]
        s = lax.dot_general(qm, kw, (((1,), (1,)), ((), ())),
                            preferred_element_type=F32)
        bias = jnp.concatenate([bias_ref[dr0 + kh] for kh in range(0, KH, 2)], axis=-1)
        logits = s + bias
        return logits, jnp.max(logits, axis=-1, keepdims=True)

    def probs(logits, m):
        p = jnp.exp(logits - m)
        return p.astype(BF16), jnp.sum(p, axis=-1, keepdims=True)

    def output(i, p, l):
        _, off = window(i)
        vw = v_ref[0, pl.ds(off, KH * GRID_W), :]
        o = jnp.dot(p, vw, preferred_element_type=F32) / l
        y = jnp.zeros((GRID_W, NA_WIDTH), F32)
        for hd, sel in enumerate(sels):
            y = y + jnp.where(sel, o[hd * GRID_W:(hd + 1) * GRID_W], 0.0)
        o_ref[0, i * GRID_W:(i + 1) * GRID_W, :] = y

    st_scores, st_probs = {}, {}
    for t in range(G + 2):
        if t < G:
            st_scores[t] = scores(t)
        if 0 <= t - 1 < G:
            st_probs[t - 1] = probs(*st_scores.pop(t - 1))
        if 0 <= t - 2 < G:
            output(t - 2, *st_probs.pop(t - 2))


def _natten(q, k, v, bias_pairs, *, G):
    B, S, _ = q.shape
    R = S // GRID_W
    KH = min(NA_KH, R)
    assert KH % 2 == 0 and R % G == 0
    return pl.pallas_call(
        functools.partial(_natten_kernel, R=R, KH=KH, G=G),
        grid=(B, R // G),
        in_specs=[pl.BlockSpec((1, G * GRID_W, NA_WIDTH), lambda b, r: (b, r, 0)),
                  pl.BlockSpec((1, S, NA_WIDTH), lambda b, r: (b, 0, 0)),
                  pl.BlockSpec((1, S, NA_WIDTH), lambda b, r: (b, 0, 0)),
                  _const_spec(bias_pairs.shape)],
        out_specs=pl.BlockSpec((1, G * GRID_W, NA_WIDTH), lambda b, r: (b, r, 0)),
        out_shape=jax.ShapeDtypeStruct((B, S, NA_WIDTH), F32),
        compiler_params=_params(("parallel", "arbitrary")), name="natten",
    )(q, k, v, bias_pairs)


def _natten_bias_pairs(rpb):
    c = jnp.arange(GRID_W)
    start = jnp.clip(c - NA_KW // 2, 0, GRID_W - NA_KW)
    v = c[None, :]
    inwin = (v >= start[:, None]) & (v < start[:, None] + NA_KW)
    dc = jnp.clip(v - c[:, None] + (NA_KW - 1), 0, 2 * NA_KW - 2)
    t = jnp.where(inwin[None, None], rpb[:, :, dc], MASK_VALUE)
    t = jnp.concatenate([t[:, :-1], t[:, 1:]], axis=-1).astype(F32)
    return t.transpose(1, 0, 2, 3).reshape(t.shape[1], NA_HEADS * GRID_W, 2 * GRID_W)


def _short_conv_kernel(v_ref, x1_ref, x2_ref, wv_ref, w1_ref, w2_ref, bv_ref, b1_ref, b2_ref,
                       u_ref, g1_ref, g2_ref):
    L = v_ref.shape[0]
    t = lax.broadcasted_iota(jnp.int32, v_ref.shape, 0)

    def conv(x_ref, w_ref, b_ref):
        x = x_ref[...]
        prev = jnp.where(t == 0, 0.0, pltpu.roll(x, 1, axis=0))
        nxt = jnp.where(t == L - 1, 0.0, pltpu.roll(x, L - 1, axis=0))
        w = w_ref[...]
        return w[0:1] * prev + w[1:2] * x + w[2:3] * nxt + b_ref[...]

    u_ref[...] = conv(v_ref, wv_ref, bv_ref).astype(BF16)
    g1_ref[...] = conv(x1_ref, w1_ref, b1_ref)
    g2_ref[...] = conv(x2_ref, w2_ref, b2_ref)


def _short_conv(hy, conv_w, conv_b, B):
    L = hy.shape[0]
    C = HY_WIDTH
    nj = C // LANE
    per_b = 3 * nj
    seg = lambda s: pl.BlockSpec((L, LANE), lambda b, j: (0, b * per_b + s * nj + j))
    wseg = lambda s: pl.BlockSpec((3, LANE), lambda b, j: (0, s * nj + j))
    bseg = lambda s: pl.BlockSpec((1, LANE), lambda b, j: (0, s * nj + j))
    out = pl.BlockSpec((L, LANE), lambda b, j: (0, b * nj + j))
    return pl.pallas_call(
        _short_conv_kernel, grid=(B, nj),
        in_specs=[seg(0), seg(1), seg(2), wseg(0), wseg(1), wseg(2), bseg(0), bseg(1), bseg(2)],
        out_specs=[out, out, out],
        out_shape=[jax.ShapeDtypeStruct((L, B * C), BF16),
                   jax.ShapeDtypeStruct((L, B * C), F32),
                   jax.ShapeDtypeStruct((L, B * C), F32)],
        compiler_params=_params(("parallel", "parallel")), name="hy_short_conv",
    )(hy, hy, hy, conv_w, conv_w, conv_w, conv_b, conv_b, conv_b)


def _filter_kernel(z_ref, w1_ref, b1_ref, f1_ref, w2_ref, b2_ref, f2_ref, w3_ref, dec_ref,
                   hp_ref, hm_ref):
    hi = lax.Precision.HIGHEST
    dot = lambda a, b: jnp.dot(a, b, precision=hi, preferred_element_type=F32)
    h = jnp.sin(f1_ref[...] * (dot(z_ref[...], w1_ref[...]) + b1_ref[...]))
    h = jnp.sin(f2_ref[...] * (dot(h, w2_ref[...]) + b2_ref[...]))
    h = dot(h, w3_ref[...])
    dec = dec_ref[...]
    C = HY_WIDTH
    for o in range(HY_ORDER):
        hf = h[:, (2 * o) * C:(2 * o + 1) * C] * dec
        hb = h[:, (2 * o + 1) * C:(2 * o + 2) * C] * dec
        hp_ref[:, o * C:(o + 1) * C] = (hf + hb).astype(BF16)
        hm_ref[:, o * C:(o + 1) * C] = (hf - hb).astype(BF16)


def _filters(z, w1, b1, f1, w2, b2, f2, w3, decay, *, tm):
    L = z.shape[0]
    OC = HY_ORDER * HY_WIDTH
    row = lambda w: pl.BlockSpec((tm, w), lambda i: (i, 0))
    return pl.pallas_call(
        _filter_kernel, grid=(L // tm,),
        in_specs=[row(z.shape[1]), _const_spec(w1.shape), _const_spec(b1.shape),
                  _const_spec(f1.shape), _const_spec(w2.shape), _const_spec(b2.shape),
                  _const_spec(f2.shape), _const_spec(w3.shape), row(HY_WIDTH)],
        out_specs=[row(OC), row(OC)],
        out_shape=[jax.ShapeDtypeStruct((L, OC), BF16), jax.ShapeDtypeStruct((L, OC), BF16)],
        compiler_params=_params(("parallel",)), name="hy_filters",
    )(z, w1, b1, f1, w2, b2, f2, w3, decay)


def _coef_kernel(c_ref, sp_ref, sp0_ref, hp_ref, hm_ref, skip_ref, a_ref, b_ref, cc_ref, *, L):
    i = pl.program_id(0)
    tk = c_ref.shape[0]
    inv = 1.0 / L
    hp = hp_ref[...]
    skip = skip_ref[...]
    gr = jnp.dot(c_ref[...], hp, preferred_element_type=F32) + skip
    qm = jnp.dot(sp_ref[...], hm_ref[...], preferred_element_type=F32)
    nyq = jnp.dot(sp0_ref[...], hp, preferred_element_type=F32)[0:1] + skip
    is0 = (i * tk + lax.broadcasted_iota(jnp.int32, gr.shape, 0)) == 0
    a = jnp.where(is0, 0.5 * inv, inv) * gr
    a_ref[...] = a
    b_ref[...] = jnp.where(is0, 0.0, -inv * qm)
    cc_ref[...] = jnp.where(is0, (0.5 * inv) * nyq, a)


def _coefs(cmat, spmat, hp, hm, skip, *, tk):
    L = cmat.shape[0]
    OC = hp.shape[1]
    blk = pl.BlockSpec((tk, L), lambda i: (i, 0))
    out = pl.BlockSpec((tk, OC), lambda i: (i, 0))
    shp = jax.ShapeDtypeStruct((L, OC), F32)
    return pl.pallas_call(
        functools.partial(_coef_kernel, L=L), grid=(L // tk,),
        in_specs=[blk, blk, pl.BlockSpec((16, L), lambda i: (0, 0)),
                  _const_spec(hp.shape), _const_spec(hm.shape), _const_spec(skip.shape)],
        out_specs=[out, out, out], out_shape=[shp, shp, shp],
        compiler_params=_params(("parallel",)), name="hy_coefs",
    )(cmat, spmat, spmat, hp, hm, skip)


def _fwd_dft_kernel(c_ref, sp_ref, u_ref, a_ref, b_ref, cc_ref, r1_ref, r2_ref, *, nb):
    u = u_ref[...]
    p = jnp.dot(c_ref[...], u, preferred_element_type=F32)
    q = jnp.dot(sp_ref[...], u, preferred_element_type=F32)
    tile = lambda x: jnp.concatenate([x] * nb, axis=-1) if nb > 1 else x
    a = tile(a_ref[...])
    b = tile(b_ref[...])
    cc = tile(cc_ref[...])
    r1_ref[...] = (p * a + q * b).astype(BF16)
    r2_ref[...] = (q * cc - p * b).astype(BF16)


def _fwd_dft(cmat, spmat, u, a, b, cc, order, *, tk, nb):
    L = cmat.shape[0]
    C = HY_WIDTH
    cb = nb * C
    ncb = u.shape[1] // cb
    blk = pl.BlockSpec((tk, L), lambda c, j: (j, 0))
    coef = pl.BlockSpec((tk, C), lambda c, j: (j, order))
    out = pl.BlockSpec((tk, cb), lambda c, j: (j, c))
    shp = jax.ShapeDtypeStruct(u.shape, BF16)
    return pl.pallas_call(
        functools.partial(_fwd_dft_kernel, nb=nb), grid=(ncb, L // tk),
        in_specs=[blk, blk, pl.BlockSpec((L, cb), lambda c, j: (0, c)), coef, coef, coef],
        out_specs=[out, out], out_shape=[shp, shp],
        compiler_params=_params(("parallel", "arbitrary")), name="hy_fwd_dft",
    )(cmat, spmat, u, a, b, cc)


def _inv_dft_kernel(c_ref, st_ref, r1_ref, r2_ref, g_ref, z_ref):
    y = (jnp.dot(c_ref[...], r1_ref[...], preferred_element_type=F32)
         + jnp.dot(st_ref[...], r2_ref[...], preferred_element_type=F32))
    z_ref[...] = (g_ref[...] * y).astype(z_ref.dtype)


def _inv_dft(cmat, stmat, r1, r2, gate, out_dtype, *, tm, nb):
    L = cmat.shape[0]
    cb = nb * HY_WIDTH
    ncb = r1.shape[1] // cb
    blk = pl.BlockSpec((tm, L), lambda c, i: (i, 0))
    full = pl.BlockSpec((L, cb), lambda c, i: (0, c))
    tile = pl.BlockSpec((tm, cb), lambda c, i: (i, c))
    return pl.pallas_call(
        _inv_dft_kernel, grid=(ncb, L // tm),
        in_specs=[blk, blk, full, full, tile],
        out_specs=tile, out_shape=jax.ShapeDtypeStruct(r1.shape, out_dtype),
        compiler_params=_params(("parallel", "arbitrary")), name="hy_inv_dft",
    )(cmat, stmat, r1, r2, gate)


def _hyena_features(L):
    t_idx = jnp.arange(L, dtype=F32)[:, None]
    t_norm = jnp.linspace(0.0, 1.0, L, dtype=F32)[:, None]
    bands = jnp.linspace(1e-4, HY_BANDS - 1, HY_BANDS, dtype=F32)[None, :]
    ang = 2.0 * math.pi * t_idx * bands / L
    z = jnp.concatenate([t_norm, jnp.cos(ang), jnp.sin(ang)], axis=-1)
    z = jnp.pad(z, ((0, 0), (0, LANE - HY_EMB)))
    deltas = jnp.linspace(math.log(HY_TARGET) / HY_SLOW_DECAY,
                          math.log(HY_TARGET) / HY_FAST_DECAY, HY_WIDTH, dtype=F32)
    decay = jnp.exp(-t_norm * jnp.abs(deltas)[None, :])
    return z, decay


def _mix_ffn_kernel(x_ref, ya_ref, yb_ref, yc_ref, gmix_ref, wout_ref, g2_ref, wg_ref, wu_ref,
                    wd_ref, gf_ref, o_ref, *, n_chunks, final_norm):
    gmix = gmix_ref[...]
    ca = ya_ref.shape[-1]
    cb = yb_ref.shape[-1]
    ymix = jnp.concatenate([
        _rms(ya_ref[0], gmix[:, :ca]),
        _rms(yb_ref[...], gmix[:, ca:ca + cb]),
        _rms(yc_ref[0], gmix[:, ca + cb:]),
    ], axis=-1).astype(BF16)
    x = x_ref[0] + jnp.dot(ymix, wout_ref[...], preferred_element_type=F32)
    h2 = _rms(x, g2_ref[...]).astype(BF16)
    ff = wg_ref.shape[1]
    ch = ff // n_chunks
    acc = x
    for c in range(n_chunks):
        sl = slice(c * ch, (c + 1) * ch)
        gate = jnp.dot(h2, wg_ref[:, sl], preferred_element_type=F32)
        up = jnp.dot(h2, wu_ref[:, sl], preferred_element_type=F32)
        act = (gate * jax.nn.sigmoid(gate) * up).astype(BF16)
        acc = acc + jnp.dot(act, wd_ref[sl, :], preferred_element_type=F32)
    if final_norm:
        acc = _rms(acc, gf_ref[...])
    o_ref[0] = acc


def _mix_ffn(x, ya, yb, yc, gmix, wout, g2, wg, wu, wd, gf, *, tm, final_norm):
    B, S, D = x.shape
    ns = S // tm
    tok = lambda w: pl.BlockSpec((1, tm, w), lambda i: (i // ns, i % ns, 0))
    once = lambda a: pl.BlockSpec(a.shape, lambda i: (0,) * a.ndim,
                                  pipeline_mode=pl.Buffered(1))
    n_chunks = 2 if (wg.shape[1] // 2) % LANE == 0 else 1
    return pl.pallas_call(
        functools.partial(_mix_ffn_kernel, n_chunks=n_chunks, final_norm=final_norm),
        grid=(B * ns,),
        in_specs=[tok(D), tok(ya.shape[-1]),
                  pl.BlockSpec((tm, HY_WIDTH), lambda i: (i % ns, i // ns)),
                  tok(yc.shape[-1]), once(gmix), once(wout), once(g2), once(wg), once(wu),
                  once(wd), once(gf)],
        out_specs=tok(D), out_shape=jax.ShapeDtypeStruct((B, S, D), F32),
        compiler_params=_params(("parallel",)), name="mix_ffn",
    )(x, ya, yb, yc, gmix, wout, g2, wg, wu, wd, gf)


def _rope_tables(S):
    half = MLA_ROPE // 2
    pos = jnp.arange(S, dtype=F32)
    inv = ROPE_THETA ** (-jnp.arange(0, MLA_ROPE, 2, dtype=F32) / MLA_ROPE)
    ang = pos[:, None] * inv[None, :]
    cos, sin = jnp.cos(ang), jnp.sin(ang)
    z_lo = jnp.zeros((S, MLA_NOPE), F32)
    z_hi = jnp.zeros((S, HEAD_PAD - MLA_NOPE - 2 * half), F32)
    ck = jnp.concatenate([z_lo, cos, cos, z_hi], axis=-1)
    sk = jnp.concatenate([z_lo, -sin, sin, z_hi], axis=-1)
    scale = math.log2(math.e) / math.sqrt(MLA_NOPE + MLA_ROPE)
    cq = jnp.concatenate([jnp.ones((S, MLA_NOPE), F32), cos, cos, z_hi], axis=-1) * scale
    sq = sk * scale
    return cq, sq, ck, sk


def _pad_heads(w, n_heads, width, keep):
    K = w.shape[0]
    w = w.reshape(K, n_heads, width)[:, :, :keep]
    w = jnp.pad(w, ((0, 0), (0, 0), (0, HEAD_PAD - keep)))
    return w.reshape(K, n_heads * HEAD_PAD)


def _swap_rope_cols(w, n_heads, width):
    K = w.shape[0]
    half = MLA_ROPE // 2
    w = w.reshape(K, n_heads, width)
    a = w[:, :, MLA_NOPE:MLA_NOPE + half]
    b = w[:, :, MLA_NOPE + half:MLA_NOPE + 2 * half]
    out = jnp.concatenate([jnp.zeros_like(w[:, :, :MLA_NOPE]), b, a], axis=-1)
    return out.reshape(K, n_heads * width)


def _layer_weights(w_in, w_uq, w_ukv):
    D = w_in.shape[0]
    o = MLA_Q_RANK + MLA_KV_RANK
    w_kpe = w_in[:, o:o + MLA_ROPE]
    zl = jnp.zeros((D, MLA_NOPE), F32)
    zh = jnp.zeros((D, HEAD_PAD - MLA_NOPE - MLA_ROPE), F32)
    half = MLA_ROPE // 2
    kpe_pad = jnp.concatenate([zl, w_kpe, zh], axis=-1)
    kpe_sw = jnp.concatenate([zl, w_kpe[:, half:], w_kpe[:, :half], zh], axis=-1)
    wlat = jnp.concatenate([w_in[:, :o], kpe_pad, kpe_sw], axis=-1).astype(BF16)
    o2 = o + MLA_ROPE
    why = w_in[:, o2:o2 + 3 * HY_WIDTH].astype(BF16)
    wna = w_in[:, o2 + 3 * HY_WIDTH:].astype(BF16)
    qd = MLA_NOPE + MLA_ROPE
    wq = _pad_heads(w_uq, MLA_HEADS, qd, qd).astype(BF16)
    wqs = _pad_heads(_swap_rope_cols(w_uq, MLA_HEADS, qd), MLA_HEADS, qd, qd).astype(BF16)
    kvd = MLA_NOPE + MLA_V
    wk = _pad_heads(w_ukv, MLA_HEADS, kvd, MLA_NOPE).astype(BF16)
    wv = w_ukv.reshape(-1, MLA_HEADS, kvd)[:, :, MLA_NOPE:].reshape(-1, MLA_WIDTH)
    wvt = wv.T.astype(BF16)
    return wlat, why, wna, wq, wqs, wk, wvt


def _pad2(a, rows, cols):
    return jnp.pad(a, ((0, rows - a.shape[0]), (0, cols - a.shape[1])))


def kernel(x, norm1_g, w_in, mla_q_norm_g, mla_w_uq, mla_kv_norm_g, mla_w_ukv, hy_conv_w, hy_conv_b, hy_filt_w1, hy_filt_b1, hy_filt_freq1, hy_filt_w2, hy_filt_b2, hy_filt_freq2, hy_filt_w3, hy_skip, na_rpb, mix_norm_g, w_out, norm2_g, ffn_w_gate, ffn_w_up, ffn_w_down, final_norm_g):
    B, S, D = x.shape
    depth = w_in.shape[0]
    L = S
    tm = min(512, S)
    tq = min(512, S)
    tkv = min(512, S)
    t_dft = min(256, L)
    nb = 2 if B % 2 == 0 else 1

    cq, sq, ck, sk = _rope_tables(S)
    cmat, spmat, stmat = _dft_tables(L)
    z_feat, decay = _hyena_features(L)
    row = lambda v: v.reshape(1, -1)

    for l in range(depth):
        wlat, why, wna, wq, wqs, wk, wvt = _layer_weights(w_in[l], mla_w_uq[l], mla_w_ukv[l])
        q, k, vt, hy, naq, nak, nav = _inproj(
            x, row(norm1_g[l]), wlat, why, wna, row(mla_q_norm_g[l]), wq, wqs,
            row(mla_kv_norm_g[l]), wk, wvt, cq, sq, ck, sk, tm=tm)

        y_a = _mla_attn(q, k, vt, tq=tq, tk=tkv, unroll=2)
        y_c = _natten(naq, nak, nav, _natten_bias_pairs(na_rpb[l]), G=min(8, S // GRID_W))

        u, g1, g2 = _short_conv(hy, hy_conv_w[l], row(hy_conv_b[l]), B)
        hp, hm = _filters(
            z_feat, _pad2(hy_filt_w1[l], LANE, LANE), _pad2(row(hy_filt_b1[l]), 1, LANE),
            _pad2(row(hy_filt_freq1[l]), 1, LANE), _pad2(hy_filt_w2[l], LANE, LANE),
            _pad2(row(hy_filt_b2[l]), 1, LANE), _pad2(row(hy_filt_freq2[l]), 1, LANE),
            _pad2(hy_filt_w3[l], LANE, hy_filt_w3.shape[-1]), decay, tm=tm)
        a, b, cc = _coefs(cmat, spmat, hp, hm, row(hy_skip[l]), tk=t_dft)
        r1, r2 = _fwd_dft(cmat, spmat, u, a, b, cc, 0, tk=t_dft, nb=nb)
        z1 = _inv_dft(cmat, stmat, r1, r2, g1, BF16, tm=t_dft, nb=nb)
        r1, r2 = _fwd_dft(cmat, spmat, z1, a, b, cc, 1, tk=t_dft, nb=nb)
        y_b = _inv_dft(cmat, stmat, r1, r2, g2, F32, tm=t_dft, nb=nb)

        x = _mix_ffn(x, y_a, y_b, y_c, row(mix_norm_g[l]), w_out[l].astype(BF16),
                     row(norm2_g[l]), ffn_w_gate[l].astype(BF16), ffn_w_up[l].astype(BF16),
                     ffn_w_down[l].astype(BF16), row(final_norm_g),
                     tm=tm, final_norm=(l == depth - 1))
    return x
```

```python
import functools
import math

import jax
import jax.numpy as jnp
import numpy as np
from jax import lax
from jax.experimental import pallas as pl
from jax.experimental.pallas import tpu as pltpu

F32 = jnp.float32
BF16 = jnp.bfloat16

NORM_EPS = 1e-6
MLA_HEADS = 6
MLA_NOPE = 64
MLA_ROPE = 32
MLA_V = 64
MLA_Q_RANK = 256
MLA_KV_RANK = 128
ROPE_THETA = 10000.0
HY_WIDTH = 384
HY_ORDER = 2
HY_BANDS = 8
HY_EMB = 1 + 2 * HY_BANDS
HY_FFN = 64
HY_FAST_DECAY = 0.3
HY_SLOW_DECAY = 1.5
HY_TARGET = 1e-2
NA_HEADS = 4
NA_HEAD_DIM = 64
NA_WIDTH = NA_HEADS * NA_HEAD_DIM
GRID_W = 64
NA_KH = 8
NA_KW = 16
MLA_WIDTH = MLA_HEADS * MLA_V

LANE = 128
HEAD_PAD = 128
BF16_SUBLANES = 16
MLA_VPAD = MLA_V + BF16_SUBLANES
VMEM_LIMIT = 56 * 1024 * 1024
MASK_VALUE = -1e30


def _params(sem, vmem=VMEM_LIMIT):
    return pltpu.CompilerParams(dimension_semantics=sem, vmem_limit_bytes=vmem)


def _rms(x, g):
    return x * lax.rsqrt(jnp.mean(x * x, axis=-1, keepdims=True) + NORM_EPS) * g


def _const_spec(shape):
    nd = len(shape)
    return pl.BlockSpec(shape, lambda *_: (0,) * nd)


def _dft_table_kernel(ce_ref, se_ref, co_ref, so_ref, cot_ref, sot_ref,
                      eec_ref, ees_ref, eoc_ref, eos_ref, *, L, tk):
    i = pl.program_id(0)
    n = 2 * L
    H = L // 2
    w = 2.0 * math.pi / n

    def angle(prod):
        return (prod & (n - 1)).astype(F32) * w

    @pl.when(i == 0)
    def _():
        r = lax.broadcasted_iota(jnp.int32, (tk, H), 0)
        c = lax.broadcasted_iota(jnp.int32, (tk, H), 1)
        ang_e = angle(r * (2 * c))
        ang_o = angle(r * (2 * c + 1))
        eec_ref[...] = jnp.cos(ang_e)
        ees_ref[...] = jnp.sin(ang_e)
        eoc_ref[...] = jnp.cos(ang_o)
        eos_ref[...] = jnp.sin(ang_o)

    k0 = i * tk
    c1 = lax.broadcasted_iota(jnp.int32, (1, H), 1)

    def rotate(a0, ec_ref, es_ref):
        ca, sa = jnp.cos(a0), jnp.sin(a0)
        ec, es = ec_ref[...], es_ref[...]
        return (ca * ec - sa * es).astype(BF16), (sa * ec + ca * es).astype(BF16)

    ce_ref[...], se_ref[...] = rotate(angle(k0 * (2 * c1)), eec_ref, ees_ref)
    co_ref[...], so_ref[...] = rotate(angle(k0 * (2 * c1 + 1)), eoc_ref, eos_ref)
    cot_ref[...], sot_ref[...] = rotate(angle(c1 * (2 * k0 + 1)), eec_ref, ees_ref)


def _dft_tables(L):
    assert L & (L - 1) == 0, "sequence length must be a power of two"
    H = L // 2
    tk = min(128, H)
    spec = pl.BlockSpec((tk, H), lambda i: (i, 0))
    shp = jax.ShapeDtypeStruct((H, H), BF16)
    return pl.pallas_call(
        functools.partial(_dft_table_kernel, L=L, tk=tk),
        grid=(H // tk,),
        out_specs=[spec] * 6,
        out_shape=[shp] * 6,
        scratch_shapes=[pltpu.VMEM((tk, H), F32)] * 4,
        compiler_params=_params(("arbitrary",)),
        name="dft_tables",
    )()


def _inproj_kernel(x_ref, g1_ref, wlat_ref, why_ref, wna_ref, gq_ref, wq_ref, wqs_ref,
                   gkv_ref, wk_ref, wvt_ref, cq_ref, sq_ref, ck_ref, sk_ref,
                   q_ref, k_ref, vt_ref, hy_ref, naq_ref, nak_ref, nav_ref, hy_scr):
    x = x_ref[0]
    h = _rms(x, g1_ref[...]).astype(BF16)
    lat = jnp.dot(h, wlat_ref[...], preferred_element_type=F32)
    hy = jnp.dot(h, why_ref[...], preferred_element_type=F32)
    half = hy.shape[0] // 2
    for j in range(hy.shape[1] // LANE):
        sl = slice(j * LANE, (j + 1) * LANE)
        hy_scr[j] = hy[:, sl]
        hy_ref[0, :, sl] = hy_scr[j, pl.ds(0, half, stride=2), :]
        hy_ref[1, :, sl] = hy_scr[j, pl.ds(1, half, stride=2), :]
    na = jnp.dot(h, wna_ref[...], preferred_element_type=F32)
    naq_ref[0] = (na[:, :NA_WIDTH] * (1.0 / math.sqrt(NA_HEAD_DIM))).astype(BF16)
    nak_ref[0] = na[:, NA_WIDTH:2 * NA_WIDTH].astype(BF16)
    nav_ref[0] = na[:, 2 * NA_WIDTH:].astype(BF16)

    c_q = lat[:, :MLA_Q_RANK]
    c_kv = lat[:, MLA_Q_RANK:MLA_Q_RANK + MLA_KV_RANK]
    o = MLA_Q_RANK + MLA_KV_RANK
    kpe = lat[:, o:o + HEAD_PAD]
    kpe_sw = lat[:, o + HEAD_PAD:o + 2 * HEAD_PAD]

    cqn = _rms(c_q, gq_ref[...]).astype(BF16)
    qf = jnp.dot(cqn, wq_ref[...], preferred_element_type=F32)
    qs = jnp.dot(cqn, wqs_ref[...], preferred_element_type=F32)
    cq = cq_ref[...]
    sq = sq_ref[...]
    ckvn = _rms(c_kv, gkv_ref[...])
    kf = jnp.dot(ckvn.astype(BF16), wk_ref[...], preferred_element_type=F32)
    kpe_r = kpe * ck_ref[...] + kpe_sw * sk_ref[...]
    for hd in range(MLA_HEADS):
        sl = slice(hd * HEAD_PAD, (hd + 1) * HEAD_PAD)
        q_ref[0, :, sl] = (qf[:, sl] * cq + qs[:, sl] * sq).astype(BF16)
        k_ref[0, :, sl] = (kf[:, sl] + kpe_r).astype(BF16)
    vt = jnp.dot(wvt_ref[...], ckvn.T.astype(BF16), preferred_element_type=F32).astype(BF16)
    ones = jnp.ones((MLA_VPAD - MLA_V, vt.shape[1]), BF16)
    for hd in range(MLA_HEADS):
        vt_ref[0, hd * MLA_VPAD:hd * MLA_VPAD + MLA_V, :] = vt[hd * MLA_V:(hd + 1) * MLA_V]
        vt_ref[0, hd * MLA_VPAD + MLA_V:(hd + 1) * MLA_VPAD, :] = ones


def _inproj(x, g1, wlat, why, wna, gq, wq, wqs, gkv, wk, wvt, cq, sq, ck, sk, *, tm):
    B, S, D = x.shape
    ns = S // tm
    hyw = why.shape[1]
    qw = MLA_HEADS * HEAD_PAD
    tok = lambda w: pl.BlockSpec((1, tm, w), lambda i: (i // ns, i % ns, 0))
    tab = pl.BlockSpec((tm, HEAD_PAD), lambda i: (i % ns, 0))
    in_specs = [tok(D), _const_spec(g1.shape), _const_spec(wlat.shape), _const_spec(why.shape),
                _const_spec(wna.shape), _const_spec(gq.shape), _const_spec(wq.shape),
                _const_spec(wqs.shape), _const_spec(gkv.shape), _const_spec(wk.shape),
                _const_spec(wvt.shape), tab, tab, tab, tab]
    out_specs = [tok(qw), tok(qw),
                 pl.BlockSpec((1, MLA_HEADS * MLA_VPAD, tm), lambda i: (i // ns, 0, i % ns)),
                 pl.BlockSpec((2, tm // 2, hyw), lambda i: (0, i % ns, i // ns)),
                 tok(NA_WIDTH), tok(NA_WIDTH), tok(NA_WIDTH)]
    out_shape = [jax.ShapeDtypeStruct((B, S, qw), BF16),
                 jax.ShapeDtypeStruct((B, S, qw), BF16),
                 jax.ShapeDtypeStruct((B, MLA_HEADS * MLA_VPAD, S), BF16),
                 jax.ShapeDtypeStruct((2, S // 2, B * hyw), F32),
                 jax.ShapeDtypeStruct((B, S, NA_WIDTH), BF16),
                 jax.ShapeDtypeStruct((B, S, NA_WIDTH), BF16),
                 jax.ShapeDtypeStruct((B, S, NA_WIDTH), BF16)]
    return pl.pallas_call(
        _inproj_kernel, grid=(B * ns,), in_specs=in_specs, out_specs=out_specs,
        out_shape=out_shape, scratch_shapes=[pltpu.VMEM((hyw // LANE, tm, LANE), F32)],
        compiler_params=_params(("parallel",)), name="inproj",
    )(x, g1, wlat, why, wna, gq, wq, wqs, gkv, wk, wvt, cq, sq, ck, sk)


def _mla_attn_kernel(q_ref, k_ref, vt_ref, o_ref, m_ref, acc_ref, *, tk, unroll):
    S = k_ref.shape[1]
    nk = S // tk
    m_ref[...] = jnp.full(m_ref.shape, -jnp.inf, F32)
    acc_ref[...] = jnp.zeros(acc_ref.shape, F32)

    def body(j, carry):
        base = j * (unroll * tk)

        def scores(c, hd):
            off = pl.multiple_of(base + c * tk, tk)
            sl = slice(hd * HEAD_PAD, (hd + 1) * HEAD_PAD)
            kj = k_ref[0, pl.ds(off, tk), sl]
            s = lax.dot_general(kj, q_ref[0, :, sl], (((1,), (1,)), ((), ())),
                                preferred_element_type=F32)
            m_old = m_ref[hd]
            m_new = jnp.maximum(m_old, jnp.max(s, axis=0, keepdims=True))
            m_ref[hd] = m_new
            return s, m_old, m_new

        def probs(s, m_old, m_new):
            return jnp.exp2(s - m_new).astype(BF16), jnp.exp2(m_old - m_new)

        def accumulate(c, hd, p, alpha):
            off = pl.multiple_of(base + c * tk, tk)
            vj = vt_ref[0, hd * MLA_VPAD:(hd + 1) * MLA_VPAD, pl.ds(off, tk)]
            acc_ref[hd] = alpha * acc_ref[hd] + jnp.dot(vj, p, preferred_element_type=F32)

        items = [(c, hd) for c in range(unroll) for hd in range(MLA_HEADS)]
        n = len(items)
        st_scores, st_probs = {}, {}
        for t in range(n + 2):
            if t < n:
                st_scores[t] = scores(*items[t])
            if 0 <= t - 1 < n:
                st_probs[t - 1] = probs(*st_scores.pop(t - 1))
            if 0 <= t - 2 < n:
                accumulate(*items[t - 2], *st_probs.pop(t - 2))
        return carry

    lax.fori_loop(0, nk // unroll, body, 0)
    outs = []
    for hd in range(MLA_HEADS):
        acc = acc_ref[hd]
        outs.append(acc[:MLA_V] / acc[MLA_V:MLA_V + 1])
    o_ref[0] = jnp.concatenate(outs, axis=0).T


def _mla_attn(q, k, vt, *, tq, tk, unroll):
    B, S, qw = q.shape
    vrows = vt.shape[1]
    assert S % (tk * unroll) == 0
    return pl.pallas_call(
        functools.partial(_mla_attn_kernel, tk=tk, unroll=unroll),
        grid=(B, S // tq),
        in_specs=[pl.BlockSpec((1, tq, qw), lambda b, i: (b, i, 0)),
                  pl.BlockSpec((1, S, qw), lambda b, i: (b, 0, 0)),
                  pl.BlockSpec((1, vrows, S), lambda b, i: (b, 0, 0))],
        out_specs=pl.BlockSpec((1, tq, MLA_WIDTH), lambda b, i: (b, i, 0)),
        out_shape=jax.ShapeDtypeStruct((B, S, MLA_WIDTH), F32),
        scratch_shapes=[pltpu.VMEM((MLA_HEADS, 1, tq), F32),
                        pltpu.VMEM((MLA_HEADS, MLA_VPAD, tq), F32)],
        compiler_params=_params(("parallel", "arbitrary")), name="mla_attn",
    )(q, k, vt)


def _natten_kernel(q_ref, k_ref, v_ref, bias_ref, o_ref, *, R, KH, G):
    col_head = lax.broadcasted_iota(jnp.int32, (1, NA_WIDTH), 1) // NA_HEAD_DIM
    sels = [col_head == hd for hd in range(NA_HEADS)]

    def window(i):
        r = pl.program_id(1) * G + i
        start = jnp.clip(r - KH // 2, 0, R - KH)
        return start - r + (NA_KH - 1), pl.multiple_of(start * GRID_W, GRID_W)

    def scores(i):
        dr0, off = window(i)
        q = q_ref[0, i * GRID_W:(i + 1) * GRID_W, :]
        qm = jnp.concatenate([jnp.where(sel, q, jnp.zeros_like(q)) for sel in sels], axis=0)
        kw = k_ref[0, pl.ds(off, KH * GRID_W), :]
        s = lax.dot_general(qm, kw, (((1,), (1,)), ((), ())),
                            preferred_element_type=F32)
        bias = jnp.concatenate([bias_ref[dr0 + kh] for kh in range(0, KH, 2)], axis=-1)
        logits = s + bias
        return logits, jnp.max(logits, axis=-1, keepdims=True)

    def probs(logits, m):
        p = jnp.exp(logits - m)
        return p.astype(BF16), jnp.sum(p, axis=-1, keepdims=True)

    def output(i, p, l):
        _, off = window(i)
        vw = v_ref[0, pl.ds(off, KH * GRID_W), :]
        o = jnp.dot(p, vw, preferred_element_type=F32) / l
        y = jnp.zeros((GRID_W, NA_WIDTH), F32)
        for hd, sel in enumerate(sels):
            y = y + jnp.where(sel, o[hd * GRID_W:(hd + 1) * GRID_W], 0.0)
        o_ref[0, i * GRID_W:(i + 1) * GRID_W, :] = y

    st_scores, st_probs = {}, {}
    for t in range(G + 2):
        if t < G:
            st_scores[t] = scores(t)
        if 0 <= t - 1 < G:
            st_probs[t - 1] = probs(*st_scores.pop(t - 1))
        if 0 <= t - 2 < G:
            output(t - 2, *st_probs.pop(t - 2))


def _natten(q, k, v, bias_pairs, *, G):
    B, S, _ = q.shape
    R = S // GRID_W
    KH = min(NA_KH, R)
    assert KH % 2 == 0 and R % G == 0
    return pl.pallas_call(
        functools.partial(_natten_kernel, R=R, KH=KH, G=G),
        grid=(B, R // G),
        in_specs=[pl.BlockSpec((1, G * GRID_W, NA_WIDTH), lambda b, r: (b, r, 0)),
                  pl.BlockSpec((1, S, NA_WIDTH), lambda b, r: (b, 0, 0)),
                  pl.BlockSpec((1, S, NA_WIDTH), lambda b, r: (b, 0, 0)),
                  _const_spec(bias_pairs.shape)],
        out_specs=pl.BlockSpec((1, G * GRID_W, NA_WIDTH), lambda b, r: (b, r, 0)),
        out_shape=jax.ShapeDtypeStruct((B, S, NA_WIDTH), F32),
        compiler_params=_params(("parallel", "arbitrary")), name="natten",
    )(q, k, v, bias_pairs)


def _natten_bias_pairs(rpb):
    c = np.arange(GRID_W)
    start = np.clip(c - NA_KW // 2, 0, GRID_W - NA_KW)
    v = c[None, :]
    inwin = (v >= start[:, None]) & (v < start[:, None] + NA_KW)
    dc = v - c[:, None] + (NA_KW - 1)
    onehot = (dc[:, :, None] == np.arange(2 * NA_KW - 1)[None, None, :]) & inwin[:, :, None]
    t = jnp.einsum('hdj,wvj->hdwv', rpb.astype(F32), jnp.asarray(onehot, F32),
                   precision=lax.Precision.HIGHEST)
    t = jnp.where(jnp.asarray(inwin)[None, None], t, MASK_VALUE)
    t = jnp.concatenate([t[:, :-1], t[:, 1:]], axis=-1).astype(F32)
    return t.transpose(1, 0, 2, 3).reshape(t.shape[1], NA_HEADS * GRID_W, 2 * GRID_W)


def _alt_sign_sum(x):
    odd = (lax.broadcasted_iota(jnp.int32, x.shape, 0) & 1) == 1
    xf = x.astype(F32)
    return jnp.sum(jnp.where(odd, -xf, xf), axis=0, keepdims=True)


def _short_conv_kernel(v_ref, x1_ref, x2_ref, wv_ref, w1_ref, w2_ref, bv_ref, b1_ref, b2_ref,
                       u_ref, g1_ref, g2_ref):
    H = v_ref.shape[1]
    i = lax.broadcasted_iota(jnp.int32, v_ref.shape[1:], 0)

    def conv(x_ref, w_ref, b_ref, o_ref):
        xe = x_ref[0]
        xo = x_ref[1]
        xo_prev = jnp.where(i == 0, 0.0, pltpu.roll(xo, 1, axis=0))
        xe_next = jnp.where(i == H - 1, 0.0, pltpu.roll(xe, H - 1, axis=0))
        w = w_ref[...]
        b = b_ref[...]
        o_ref[0] = (w[0:1] * xo_prev + w[1:2] * xe + w[2:3] * xo + b).astype(o_ref.dtype)
        o_ref[1] = (w[0:1] * xe + w[1:2] * xo + w[2:3] * xe_next + b).astype(o_ref.dtype)

    conv(v_ref, wv_ref, bv_ref, u_ref)
    conv(x1_ref, w1_ref, b1_ref, g1_ref)
    conv(x2_ref, w2_ref, b2_ref, g2_ref)


def _short_conv(hy, conv_w, conv_b, B):
    _, H, _ = hy.shape
    C = HY_WIDTH
    nj = C // LANE
    per_b = 3 * nj
    seg = lambda s: pl.BlockSpec((2, H, LANE), lambda b, j: (0, 0, b * per_b + s * nj + j))
    wseg = lambda s: pl.BlockSpec((3, LANE), lambda b, j: (0, s * nj + j))
    bseg = lambda s: pl.BlockSpec((1, LANE), lambda b, j: (0, s * nj + j))
    out = pl.BlockSpec((2, H, LANE), lambda b, j: (0, 0, b * nj + j))
    return pl.pallas_call(
        _short_conv_kernel, grid=(B, nj),
        in_specs=[seg(0), seg(1), seg(2), wseg(0), wseg(1), wseg(2), bseg(0), bseg(1), bseg(2)],
        out_specs=[out, out, out],
        out_shape=[jax.ShapeDtypeStruct((2, H, B * C), BF16),
                   jax.ShapeDtypeStruct((2, H, B * C), F32),
                   jax.ShapeDtypeStruct((2, H, B * C), F32)],
        compiler_params=_params(("parallel", "parallel")), name="hy_short_conv",
    )(hy, hy, hy, conv_w, conv_w, conv_w, conv_b, conv_b, conv_b)


def _filter_kernel(z_ref, w1_ref, b1_ref, f1_ref, w2_ref, b2_ref, f2_ref, w3_ref, dec_ref,
                   hp_ref, hm_ref):
    hi = lax.Precision.HIGHEST
    dot = lambda a, b: jnp.dot(a, b, precision=hi, preferred_element_type=F32)
    h = jnp.sin(f1_ref[...] * (dot(z_ref[...], w1_ref[...]) + b1_ref[...]))
    h = jnp.sin(f2_ref[...] * (dot(h, w2_ref[...]) + b2_ref[...]))
    h = dot(h, w3_ref[...])
    dec = dec_ref[...]
    C = HY_WIDTH
    for o in range(HY_ORDER):
        hf = h[:, (2 * o) * C:(2 * o + 1) * C] * dec
        hb = h[:, (2 * o + 1) * C:(2 * o + 2) * C] * dec
        hp_ref[:, o * C:(o + 1) * C] = (hf + hb).astype(BF16)
        hm_ref[:, o * C:(o + 1) * C] = (hf - hb).astype(BF16)


def _filters(z, w1, b1, f1, w2, b2, f2, w3, decay, *, tm):
    L = z.shape[0]
    OC = HY_ORDER * HY_WIDTH
    row = lambda w: pl.BlockSpec((tm, w), lambda i: (i, 0))
    return pl.pallas_call(
        _filter_kernel, grid=(L // tm,),
        in_specs=[row(z.shape[1]), _const_spec(w1.shape), _const_spec(b1.shape),
                  _const_spec(f1.shape), _const_spec(w2.shape), _const_spec(b2.shape),
                  _const_spec(f2.shape), _const_spec(w3.shape), row(HY_WIDTH)],
        out_specs=[row(OC), row(OC)],
        out_shape=[jax.ShapeDtypeStruct((L, OC), BF16), jax.ShapeDtypeStruct((L, OC), BF16)],
        compiler_params=_params(("parallel",)), name="hy_filters",
    )(z, w1, b1, f1, w2, b2, f2, w3, decay)


def _coef_kernel(ce_ref, se_ref, co_ref, so_ref, hp_ref, hm_ref, skip_ref,
                 alo_ref, ahi_ref, blo_ref, bhi_ref, ah_ref, bh_ref, *, L):
    i = pl.program_id(0)
    tk = ce_ref.shape[0]
    inv = 1.0 / L
    skip = skip_ref[...]
    pe = jnp.dot(ce_ref[...], hp_ref[0], preferred_element_type=F32)
    po = jnp.dot(co_ref[...], hp_ref[1], preferred_element_type=F32)
    qe = jnp.dot(se_ref[...], hm_ref[0], preferred_element_type=F32)
    qo = jnp.dot(so_ref[...], hm_ref[1], preferred_element_type=F32)
    is0 = (i * tk + lax.broadcasted_iota(jnp.int32, pe.shape, 0)) == 0
    sc = jnp.where(is0, 0.5 * inv, inv)
    alo_ref[...] = sc * (pe + po + skip)
    ahi_ref[...] = sc * (pe - po + skip)
    blo_ref[...] = -inv * (qe + qo)
    bhi_ref[...] = inv * (qe - qo)

    @pl.when(i == 0)
    def _():
        gr = _alt_sign_sum(hp_ref[0]) + skip
        gi = -_alt_sign_sum(hm_ref[1])
        ah_ref[...] = jnp.broadcast_to(inv * gr, ah_ref.shape)
        bh_ref[...] = jnp.broadcast_to(inv * gi, bh_ref.shape)


def _coefs(tabs, hp, hm, skip, *, tk):
    ce, se, co, so = tabs[:4]
    H = ce.shape[0]
    OC = hp.shape[-1]
    blk = pl.BlockSpec((tk, H), lambda i: (i, 0))
    out = pl.BlockSpec((tk, OC), lambda i: (i, 0))
    half = pl.BlockSpec((8, OC), lambda i: (0, 0))
    shp = jax.ShapeDtypeStruct((H, OC), F32)
    shp_h = jax.ShapeDtypeStruct((8, OC), F32)
    return pl.pallas_call(
        functools.partial(_coef_kernel, L=2 * H), grid=(H // tk,),
        in_specs=[blk, blk, blk, blk, _const_spec(hp.shape), _const_spec(hm.shape),
                  _const_spec(skip.shape)],
        out_specs=[out, out, out, out, half, half],
        out_shape=[shp, shp, shp, shp, shp_h, shp_h],
        compiler_params=_params(("arbitrary",)), name="hy_coefs",
    )(ce, se, co, so, hp, hm, skip)


def _fwd_dft_kernel(ce_ref, se_ref, co_ref, so_ref, u_ref, alo_ref, ahi_ref, blo_ref, bhi_ref,
                    ah_ref, bh_ref, xa_ref, xb_ref, xc_ref, xd_ref, r1h_ref, r2h_ref, *, nb):
    tile = lambda x: jnp.concatenate([x] * nb, axis=-1) if nb > 1 else x
    ue = u_ref[0]
    uo = u_ref[1]
    pe = jnp.dot(ce_ref[...], ue, preferred_element_type=F32)
    po = jnp.dot(co_ref[...], uo, preferred_element_type=F32)
    qe = jnp.dot(se_ref[...], ue, preferred_element_type=F32)
    qo = jnp.dot(so_ref[...], uo, preferred_element_type=F32)
    p_lo, p_hi, q_lo, q_hi = pe + po, pe - po, qe + qo, qo - qe
    a_lo, a_hi = tile(alo_ref[...]), tile(ahi_ref[...])
    b_lo, b_hi = tile(blo_ref[...]), tile(bhi_ref[...])
    r1_lo = p_lo * a_lo + q_lo * b_lo
    r2_lo = q_lo * a_lo - p_lo * b_lo
    r1_hi = p_hi * a_hi + q_hi * b_hi
    r2_hi = q_hi * a_hi - p_hi * b_hi
    xa_ref[...] = (r1_lo + r1_hi).astype(BF16)
    xb_ref[...] = (r2_lo - r2_hi).astype(BF16)
    xc_ref[...] = (r1_lo - r1_hi).astype(BF16)
    xd_ref[...] = (r2_lo + r2_hi).astype(BF16)

    @pl.when(pl.program_id(1) == 0)
    def _():
        ph = _alt_sign_sum(ue)
        qh = _alt_sign_sum(uo)
        a_h = tile(ah_ref[0:1])
        b_h = tile(bh_ref[0:1])
        r1h_ref[...] = jnp.broadcast_to(ph * a_h + qh * b_h, r1h_ref.shape)
        r2h_ref[...] = jnp.broadcast_to(qh * a_h - ph * b_h, r2h_ref.shape)


def _fwd_dft(tabs, u, coefs, order, *, tk, nb):
    ce, se, co, so = tabs[:4]
    alo, ahi, blo, bhi, ah, bh = coefs
    H = ce.shape[0]
    C = HY_WIDTH
    cb = nb * C
    cols = u.shape[-1]
    blk = pl.BlockSpec((tk, H), lambda c, j: (j, 0))
    coef = pl.BlockSpec((tk, C), lambda c, j: (j, order))
    half = pl.BlockSpec((8, C), lambda c, j: (0, order))
    out = pl.BlockSpec((tk, cb), lambda c, j: (j, c))
    out_h = pl.BlockSpec((8, cb), lambda c, j: (0, c))
    shp = jax.ShapeDtypeStruct((H, cols), BF16)
    shp_h = jax.ShapeDtypeStruct((8, cols), F32)
    return pl.pallas_call(
        functools.partial(_fwd_dft_kernel, nb=nb), grid=(cols // cb, H // tk),
        in_specs=[blk, blk, blk, blk, pl.BlockSpec((2, H, cb), lambda c, j: (0, 0, c)),
                  coef, coef, coef, coef, half, half],
        out_specs=[out, out, out, out, out_h, out_h],
        out_shape=[shp, shp, shp, shp, shp_h, shp_h],
        compiler_params=_params(("parallel", "arbitrary")), name="hy_fwd_dft",
    )(ce, se, co, so, u, alo, ahi, blo, bhi, ah, bh)


def _inv_dft_kernel(ce_ref, se_ref, cot_ref, sot_ref, xa_ref, xb_ref, xc_ref, xd_ref,
                    r1h_ref, r2h_ref, g_ref, z_ref):
    tm = ce_ref.shape[0]
    ye = (jnp.dot(ce_ref[...], xa_ref[...], preferred_element_type=F32)
          + jnp.dot(se_ref[...], xb_ref[...], preferred_element_type=F32))
    yo = (jnp.dot(cot_ref[...], xc_ref[...], preferred_element_type=F32)
          + jnp.dot(sot_ref[...], xd_ref[...], preferred_element_type=F32))
    rows = pl.program_id(1) * tm + lax.broadcasted_iota(jnp.int32, ye.shape, 0)
    odd = (rows & 1) == 1
    r1h = r1h_ref[0:1]
    r2h = r2h_ref[0:1]
    ye = ye + jnp.where(odd, -r1h, r1h)
    yo = yo + jnp.where(odd, -r2h, r2h)
    z_ref[0] = (g_ref[0] * ye).astype(z_ref.dtype)
    z_ref[1] = (g_ref[1] * yo).astype(z_ref.dtype)


def _inv_dft(tabs, xs, gate, out_dtype, *, tm, nb):
    ce, se, _, _, cot, sot = tabs
    xa, xb, xc, xd, r1h, r2h = xs
    H = ce.shape[0]
    cb = nb * HY_WIDTH
    cols = xa.shape[-1]
    blk = pl.BlockSpec((tm, H), lambda c, i: (i, 0))
    full = pl.BlockSpec((H, cb), lambda c, i: (0, c))
    half = pl.BlockSpec((8, cb), lambda c, i: (0, c))
    tile = pl.BlockSpec((2, tm, cb), lambda c, i: (0, i, c))
    return pl.pallas_call(
        _inv_dft_kernel, grid=(cols // cb, H // tm),
        in_specs=[blk, blk, blk, blk, full, full, full, full, half, half, tile],
        out_specs=tile, out_shape=jax.ShapeDtypeStruct((2, H, cols), out_dtype),
        compiler_params=_params(("parallel", "arbitrary")), name="hy_inv_dft",
    )(ce, se, cot, sot, xa, xb, xc, xd, r1h, r2h, gate)


def _hyena_features(L):
    t_idx = jnp.arange(L, dtype=F32)[:, None]
    t_norm = jnp.linspace(0.0, 1.0, L, dtype=F32)[:, None]
    bands = jnp.linspace(1e-4, HY_BANDS - 1, HY_BANDS, dtype=F32)[None, :]
    ang = 2.0 * math.pi * t_idx * bands / L
    z = jnp.concatenate([t_norm, jnp.cos(ang), jnp.sin(ang)], axis=-1)
    z = jnp.pad(z, ((0, 0), (0, LANE - HY_EMB)))
    deltas = jnp.linspace(math.log(HY_TARGET) / HY_SLOW_DECAY,
                          math.log(HY_TARGET) / HY_FAST_DECAY, HY_WIDTH, dtype=F32)
    decay = jnp.exp(-t_norm * jnp.abs(deltas)[None, :])
    deint = lambda a: jnp.concatenate([a[0::2], a[1::2]], axis=0)
    return deint(z), deint(decay)


def _mix_ffn_kernel(x_ref, ya_ref, yb_ref, yc_ref, gmix_ref, wout_ref, g2_ref, wg_ref, wu_ref,
                    wd_ref, gf_ref, o_ref, yb_scr, *, n_chunks, final_norm):
    gmix = gmix_ref[...]
    ca = ya_ref.shape[-1]
    cb = yb_ref.shape[-1]
    half = yb_ref.shape[1]
    for j in range(cb // LANE):
        sl = slice(j * LANE, (j + 1) * LANE)
        yb_scr[j, pl.ds(0, half, stride=2), :] = yb_ref[0, :, sl]
        yb_scr[j, pl.ds(1, half, stride=2), :] = yb_ref[1, :, sl]
    yb = jnp.concatenate([yb_scr[j] for j in range(cb // LANE)], axis=-1)
    ymix = jnp.concatenate([
        _rms(ya_ref[0], gmix[:, :ca]),
        _rms(yb, gmix[:, ca:ca + cb]),
        _rms(yc_ref[0], gmix[:, ca + cb:]),
    ], axis=-1).astype(BF16)
    x = x_ref[0] + jnp.dot(ymix, wout_ref[...], preferred_element_type=F32)
    h2 = _rms(x, g2_ref[...]).astype(BF16)
    ff = wg_ref.shape[1]
    ch = ff // n_chunks
    acc = x
    for c in range(n_chunks):
        sl = slice(c * ch, (c + 1) * ch)
        gate = jnp.dot(h2, wg_ref[:, sl], preferred_element_type=F32)
        up = jnp.dot(h2, wu_ref[:, sl], preferred_element_type=F32)
        act = (gate * jax.nn.sigmoid(gate) * up).astype(BF16)
        acc = acc + jnp.dot(act, wd_ref[sl, :], preferred_element_type=F32)
    if final_norm:
        acc = _rms(acc, gf_ref[...])
    o_ref[0] = acc


def _mix_ffn(x, ya, yb, yc, gmix, wout, g2, wg, wu, wd, gf, *, tm, final_norm):
    B, S, D = x.shape
    ns = S // tm
    tok = lambda w: pl.BlockSpec((1, tm, w), lambda i: (i // ns, i % ns, 0))
    once = lambda a: pl.BlockSpec(a.shape, lambda i: (0,) * a.ndim,
                                  pipeline_mode=pl.Buffered(1))
    n_chunks = 2 if (wg.shape[1] // 2) % LANE == 0 else 1
    return pl.pallas_call(
        functools.partial(_mix_ffn_kernel, n_chunks=n_chunks, final_norm=final_norm),
        grid=(B * ns,),
        in_specs=[tok(D), tok(ya.shape[-1]),
                  pl.BlockSpec((2, tm // 2, HY_WIDTH), lambda i: (0, i % ns, i // ns)),
                  tok(yc.shape[-1]), once(gmix), once(wout), once(g2), once(wg), once(wu),
                  once(wd), once(gf)],
        out_specs=tok(D), out_shape=jax.ShapeDtypeStruct((B, S, D), F32),
        scratch_shapes=[pltpu.VMEM((HY_WIDTH // LANE, tm, LANE), F32)],
        compiler_params=_params(("parallel",)), name="mix_ffn",
    )(x, ya, yb, yc, gmix, wout, g2, wg, wu, wd, gf)


def _rope_tables(S):
    half = MLA_ROPE // 2
    pos = jnp.arange(S, dtype=F32)
    inv = ROPE_THETA ** (-jnp.arange(0, MLA_ROPE, 2, dtype=F32) / MLA_ROPE)
    ang = pos[:, None] * inv[None, :]
    cos, sin = jnp.cos(ang), jnp.sin(ang)
    z_lo = jnp.zeros((S, MLA_NOPE), F32)
    z_hi = jnp.zeros((S, HEAD_PAD - MLA_NOPE - 2 * half), F32)
    ck = jnp.concatenate([z_lo, cos, cos, z_hi], axis=-1)
    sk = jnp.concatenate([z_lo, -sin, sin, z_hi], axis=-1)
    scale = math.log2(math.e) / math.sqrt(MLA_NOPE + MLA_ROPE)
    cq = jnp.concatenate([jnp.ones((S, MLA_NOPE), F32), cos, cos, z_hi], axis=-1) * scale
    sq = sk * scale
    return cq, sq, ck, sk


def _pad_heads(w, n_heads, width, keep):
    K = w.shape[0]
    w = w.reshape(K, n_heads, width)[:, :, :keep]
    w = jnp.pad(w, ((0, 0), (0, 0), (0, HEAD_PAD - keep)))
    return w.reshape(K, n_heads * HEAD_PAD)


def _swap_rope_cols(w, n_heads, width):
    K = w.shape[0]
    half = MLA_ROPE // 2
    w = w.reshape(K, n_heads, width)
    a = w[:, :, MLA_NOPE:MLA_NOPE + half]
    b = w[:, :, MLA_NOPE + half:MLA_NOPE + 2 * half]
    out = jnp.concatenate([jnp.zeros_like(w[:, :, :MLA_NOPE]), b, a], axis=-1)
    return out.reshape(K, n_heads * width)


def _layer_weights(w_in, w_uq, w_ukv):
    D = w_in.shape[0]
    o = MLA_Q_RANK + MLA_KV_RANK
    w_kpe = w_in[:, o:o + MLA_ROPE]
    zl = jnp.zeros((D, MLA_NOPE), F32)
    zh = jnp.zeros((D, HEAD_PAD - MLA_NOPE - MLA_ROPE), F32)
    half = MLA_ROPE // 2
    kpe_pad = jnp.concatenate([zl, w_kpe, zh], axis=-1)
    kpe_sw = jnp.concatenate([zl, w_kpe[:, half:], w_kpe[:, :half], zh], axis=-1)
    wlat = jnp.concatenate([w_in[:, :o], kpe_pad, kpe_sw], axis=-1).astype(BF16)
    o2 = o + MLA_ROPE
    why = w_in[:, o2:o2 + 3 * HY_WIDTH].astype(BF16)
    wna = w_in[:, o2 + 3 * HY_WIDTH:].astype(BF16)
    qd = MLA_NOPE + MLA_ROPE
    wq = _pad_heads(w_uq, MLA_HEADS, qd, qd).astype(BF16)
    wqs = _pad_heads(_swap_rope_cols(w_uq, MLA_HEADS, qd), MLA_HEADS, qd, qd).astype(BF16)
    kvd = MLA_NOPE + MLA_V
    wk = _pad_heads(w_ukv, MLA_HEADS, kvd, MLA_NOPE).astype(BF16)
    wv = w_ukv.reshape(-1, MLA_HEADS, kvd)[:, :, MLA_NOPE:].reshape(-1, MLA_WIDTH)
    wvt = wv.T.astype(BF16)
    return wlat, why, wna, wq, wqs, wk, wvt


def _pad2(a, rows, cols):
    return jnp.pad(a, ((0, rows - a.shape[0]), (0, cols - a.shape[1])))


def kernel(x, norm1_g, w_in, mla_q_norm_g, mla_w_uq, mla_kv_norm_g, mla_w_ukv, hy_conv_w, hy_conv_b, hy_filt_w1, hy_filt_b1, hy_filt_freq1, hy_filt_w2, hy_filt_b2, hy_filt_freq2, hy_filt_w3, hy_skip, na_rpb, mix_norm_g, w_out, norm2_g, ffn_w_gate, ffn_w_up, ffn_w_down, final_norm_g):
    B, S, D = x.shape
    depth = w_in.shape[0]
    L = S
    tm = min(512, S)
    tq = min(512, S)
    tkv = min(512, S)
    t_dft = min(256, L // 2)
    nb = 2 if B % 2 == 0 else 1

    cq, sq, ck, sk = _rope_tables(S)
    tabs = _dft_tables(L)
    z_feat, decay = _hyena_features(L)
    row = lambda v: v.reshape(1, -1)

    for l in range(depth):
        wlat, why, wna, wq, wqs, wk, wvt = _layer_weights(w_in[l], mla_w_uq[l], mla_w_ukv[l])
        q, k, vt, hy, naq, nak, nav = _inproj(
            x, row(norm1_g[l]), wlat, why, wna, row(mla_q_norm_g[l]), wq, wqs,
            row(mla_kv_norm_g[l]), wk, wvt, cq, sq, ck, sk, tm=tm)

        y_a = _mla_attn(q, k, vt, tq=tq, tk=tkv, unroll=2)
        y_c = _natten(naq, nak, nav, _natten_bias_pairs(na_rpb[l]), G=min(8, S // GRID_W))

        u, g1, g2 = _short_conv(hy, hy_conv_w[l], row(hy_conv_b[l]), B)
        hp, hm = _filters(
            z_feat, _pad2(hy_filt_w1[l], LANE, LANE), _pad2(row(hy_filt_b1[l]), 1, LANE),
            _pad2(row(hy_filt_freq1[l]), 1, LANE), _pad2(hy_filt_w2[l], LANE, LANE),
            _pad2(row(hy_filt_b2[l]), 1, LANE), _pad2(row(hy_filt_freq2[l]), 1, LANE),
            _pad2(hy_filt_w3[l], LANE, hy_filt_w3.shape[-1]), decay, tm=tm)
        halves = lambda a: a.reshape(2, L // 2, a.shape[-1])
        coefs = _coefs(tabs, halves(hp), halves(hm), row(hy_skip[l]), tk=t_dft)
        xs = _fwd_dft(tabs, u, coefs, 0, tk=t_dft, nb=nb)
        z1 = _inv_dft(tabs, xs, g1, BF16, tm=t_dft, nb=nb)
        xs = _fwd_dft(tabs, z1, coefs, 1, tk=t_dft, nb=nb)
        y_b = _inv_dft(tabs, xs, g2, F32, tm=t_dft, nb=nb)

        x = _mix_ffn(x, y_a, y_b, y_c, row(mix_norm_g[l]), w_out[l].astype(BF16),
                     row(norm2_g[l]), ffn_w_gate[l].astype(BF16), ffn_w_up[l].astype(BF16),
                     ffn_w_down[l].astype(BF16), row(final_norm_g),
                     tm=tm, final_norm=(l == depth - 1))
    return x
```

```python
import functools
import math

import jax
import jax.numpy as jnp
import numpy as np
from jax import lax
from jax.experimental import pallas as pl
from jax.experimental.pallas import tpu as pltpu

F32 = jnp.float32
BF16 = jnp.bfloat16

NORM_EPS = 1e-6
MLA_HEADS = 6
MLA_NOPE = 64
MLA_ROPE = 32
MLA_V = 64
MLA_Q_RANK = 256
MLA_KV_RANK = 128
ROPE_THETA = 10000.0
HY_WIDTH = 384
HY_ORDER = 2
HY_BANDS = 8
HY_EMB = 1 + 2 * HY_BANDS
HY_FFN = 64
HY_FAST_DECAY = 0.3
HY_SLOW_DECAY = 1.5
HY_TARGET = 1e-2
NA_HEADS = 4
NA_HEAD_DIM = 64
NA_WIDTH = NA_HEADS * NA_HEAD_DIM
GRID_W = 64
NA_KH = 8
NA_KW = 16
MLA_WIDTH = MLA_HEADS * MLA_V

LANE = 128
HEAD_PAD = 128
F32_SUBLANES = 8
BF16_SUBLANES = 16
MLA_VPAD = MLA_V + BF16_SUBLANES
VMEM_LIMIT = 56 * 1024 * 1024
MASK_VALUE = -1e30
DFT_COL_CHUNK = 256


def _params(sem, vmem=VMEM_LIMIT):
    return pltpu.CompilerParams(dimension_semantics=sem, vmem_limit_bytes=vmem)


def _rms(x, g):
    return x * lax.rsqrt(jnp.mean(x * x, axis=-1, keepdims=True) + NORM_EPS) * g


def _const_spec(shape):
    nd = len(shape)
    return pl.BlockSpec(shape, lambda *_: (0,) * nd)


def _dft_table_kernel(ce_ref, se_ref, co_ref, so_ref, cot_ref, sot_ref,
                      eec_ref, ees_ref, eoc_ref, eos_ref, *, L, tk):
    i = pl.program_id(0)
    n = 2 * L
    H = L // 2
    w = 2.0 * math.pi / n

    def angle(prod):
        return (prod & (n - 1)).astype(F32) * w

    @pl.when(i == 0)
    def _():
        r = lax.broadcasted_iota(jnp.int32, (tk, H), 0)
        c = lax.broadcasted_iota(jnp.int32, (tk, H), 1)
        ang_e = angle(r * (2 * c))
        ang_o = angle(r * (2 * c + 1))
        eec_ref[...] = jnp.cos(ang_e)
        ees_ref[...] = jnp.sin(ang_e)
        eoc_ref[...] = jnp.cos(ang_o)
        eos_ref[...] = jnp.sin(ang_o)

    k0 = i * tk
    c1 = lax.broadcasted_iota(jnp.int32, (1, H), 1)

    def rotate(a0, ec_ref, es_ref):
        ca, sa = jnp.cos(a0), jnp.sin(a0)
        ec, es = ec_ref[...], es_ref[...]
        return (ca * ec - sa * es).astype(BF16), (sa * ec + ca * es).astype(BF16)

    ce_ref[...], se_ref[...] = rotate(angle(k0 * (2 * c1)), eec_ref, ees_ref)
    co_ref[...], so_ref[...] = rotate(angle(k0 * (2 * c1 + 1)), eoc_ref, eos_ref)
    cot_ref[...], sot_ref[...] = rotate(angle(c1 * (2 * k0 + 1)), eec_ref, ees_ref)


def _dft_tables(L):
    assert L & (L - 1) == 0, "sequence length must be a power of two"
    H = L // 2
    tk = min(128, H)
    spec = pl.BlockSpec((tk, H), lambda i: (i, 0))
    shp = jax.ShapeDtypeStruct((H, H), BF16)
    return pl.pallas_call(
        functools.partial(_dft_table_kernel, L=L, tk=tk),
        grid=(H // tk,),
        out_specs=[spec] * 6,
        out_shape=[shp] * 6,
        scratch_shapes=[pltpu.VMEM((tk, H), F32)] * 4,
        compiler_params=_params(("arbitrary",)),
        name="dft_tables",
    )()


def _inproj_kernel(x_ref, g1_ref, wlat_ref, why_ref, wna_ref, gq_ref, wq_ref, wqs_ref,
                   gkv_ref, wk_ref, wvt_ref, cq_ref, sq_ref, ck_ref, sk_ref,
                   q_ref, k_ref, vt_ref, hy_ref, naq_ref, nak_ref, nav_ref, hy_scr):
    x = x_ref[0]
    h = _rms(x, g1_ref[...]).astype(BF16)
    lat = jnp.dot(h, wlat_ref[...], preferred_element_type=F32)
    hy = jnp.dot(h, why_ref[...], preferred_element_type=F32)
    half = hy.shape[0] // 2
    for j in range(hy.shape[1] // LANE):
        sl = slice(j * LANE, (j + 1) * LANE)
        hy_scr[j] = hy[:, sl]
        hy_ref[0, :, sl] = hy_scr[j, pl.ds(0, half, stride=2), :]
        hy_ref[1, :, sl] = hy_scr[j, pl.ds(1, half, stride=2), :]
    na = jnp.dot(h, wna_ref[...], preferred_element_type=F32)
    naq_ref[0] = (na[:, :NA_WIDTH] * (1.0 / math.sqrt(NA_HEAD_DIM))).astype(BF16)
    nak_ref[0] = na[:, NA_WIDTH:2 * NA_WIDTH].astype(BF16)
    nav_ref[0] = na[:, 2 * NA_WIDTH:].astype(BF16)

    c_q = lat[:, :MLA_Q_RANK]
    c_kv = lat[:, MLA_Q_RANK:MLA_Q_RANK + MLA_KV_RANK]
    o = MLA_Q_RANK + MLA_KV_RANK
    kpe = lat[:, o:o + HEAD_PAD]
    kpe_sw = lat[:, o + HEAD_PAD:o + 2 * HEAD_PAD]

    cqn = _rms(c_q, gq_ref[...]).astype(BF16)
    qf = jnp.dot(cqn, wq_ref[...], preferred_element_type=F32)
    qs = jnp.dot(cqn, wqs_ref[...], preferred_element_type=F32)
    cq = cq_ref[...]
    sq = sq_ref[...]
    ckvn = _rms(c_kv, gkv_ref[...])
    kf = jnp.dot(ckvn.astype(BF16), wk_ref[...], preferred_element_type=F32)
    kpe_r = kpe * ck_ref[...] + kpe_sw * sk_ref[...]
    for hd in range(MLA_HEADS):
        sl = slice(hd * HEAD_PAD, (hd + 1) * HEAD_PAD)
        q_ref[0, :, sl] = (qf[:, sl] * cq + qs[:, sl] * sq).astype(BF16)
        k_ref[0, :, sl] = (kf[:, sl] + kpe_r).astype(BF16)
    vt = jnp.dot(wvt_ref[...], ckvn.T.astype(BF16), preferred_element_type=F32).astype(BF16)
    ones = jnp.ones((MLA_VPAD - MLA_V, vt.shape[1]), BF16)
    for hd in range(MLA_HEADS):
        vt_ref[0, hd * MLA_VPAD:hd * MLA_VPAD + MLA_V, :] = vt[hd * MLA_V:(hd + 1) * MLA_V]
        vt_ref[0, hd * MLA_VPAD + MLA_V:(hd + 1) * MLA_VPAD, :] = ones


def _inproj(x, g1, wlat, why, wna, gq, wq, wqs, gkv, wk, wvt, cq, sq, ck, sk, *, tm):
    B, S, D = x.shape
    ns = S // tm
    hyw = why.shape[1]
    qw = MLA_HEADS * HEAD_PAD
    tok = lambda w: pl.BlockSpec((1, tm, w), lambda i: (i // ns, i % ns, 0))
    tab = pl.BlockSpec((tm, HEAD_PAD), lambda i: (i % ns, 0))
    in_specs = [tok(D), _const_spec(g1.shape), _const_spec(wlat.shape), _const_spec(why.shape),
                _const_spec(wna.shape), _const_spec(gq.shape), _const_spec(wq.shape),
                _const_spec(wqs.shape), _const_spec(gkv.shape), _const_spec(wk.shape),
                _const_spec(wvt.shape), tab, tab, tab, tab]
    out_specs = [tok(qw), tok(qw),
                 pl.BlockSpec((1, MLA_HEADS * MLA_VPAD, tm), lambda i: (i // ns, 0, i % ns)),
                 pl.BlockSpec((2, tm // 2, hyw), lambda i: (0, i % ns, i // ns)),
                 tok(NA_WIDTH), tok(NA_WIDTH), tok(NA_WIDTH)]
    out_shape = [jax.ShapeDtypeStruct((B, S, qw), BF16),
                 jax.ShapeDtypeStruct((B, S, qw), BF16),
                 jax.ShapeDtypeStruct((B, MLA_HEADS * MLA_VPAD, S), BF16),
                 jax.ShapeDtypeStruct((2, S // 2, B * hyw), F32),
                 jax.ShapeDtypeStruct((B, S, NA_WIDTH), BF16),
                 jax.ShapeDtypeStruct((B, S, NA_WIDTH), BF16),
                 jax.ShapeDtypeStruct((B, S, NA_WIDTH), BF16)]
    return pl.pallas_call(
        _inproj_kernel, grid=(B * ns,), in_specs=in_specs, out_specs=out_specs,
        out_shape=out_shape, scratch_shapes=[pltpu.VMEM((hyw // LANE, tm, LANE), F32)],
        compiler_params=_params(("parallel",)), name="inproj",
    )(x, g1, wlat, why, wna, gq, wq, wqs, gkv, wk, wvt, cq, sq, ck, sk)


def _mla_attn_kernel(q_ref, k_ref, vt_ref, o_ref, m_ref, acc_ref, *, tk, unroll):
    S = k_ref.shape[1]
    nk = S // tk
    m_ref[...] = jnp.full(m_ref.shape, -jnp.inf, F32)
    acc_ref[...] = jnp.zeros(acc_ref.shape, F32)

    def body(j, carry):
        base = j * (unroll * tk)

        def scores(c, hd):
            off = pl.multiple_of(base + c * tk, tk)
            sl = slice(hd * HEAD_PAD, (hd + 1) * HEAD_PAD)
            kj = k_ref[0, pl.ds(off, tk), sl]
            s = lax.dot_general(kj, q_ref[0, :, sl], (((1,), (1,)), ((), ())),
                                preferred_element_type=F32)
            m_old = m_ref[hd]
            m_new = jnp.maximum(m_old, jnp.max(s, axis=0, keepdims=True))
            m_ref[hd] = m_new
            return s, m_old, m_new

        def probs(s, m_old, m_new):
            return jnp.exp2(s - m_new).astype(BF16), jnp.exp2(m_old - m_new)

        def accumulate(c, hd, p, alpha):
            off = pl.multiple_of(base + c * tk, tk)
            vj = vt_ref[0, hd * MLA_VPAD:(hd + 1) * MLA_VPAD, pl.ds(off, tk)]
            acc_ref[hd] = alpha * acc_ref[hd] + jnp.dot(vj, p, preferred_element_type=F32)

        items = [(c, hd) for c in range(unroll) for hd in range(MLA_HEADS)]
        n = len(items)
        st_scores, st_probs = {}, {}
        for t in range(n + 2):
            if t < n:
                st_scores[t] = scores(*items[t])
            if 0 <= t - 1 < n:
                st_probs[t - 1] = probs(*st_scores.pop(t - 1))
            if 0 <= t - 2 < n:
                accumulate(*items[t - 2], *st_probs.pop(t - 2))
        return carry

    lax.fori_loop(0, nk // unroll, body, 0)
    outs = []
    for hd in range(MLA_HEADS):
        acc = acc_ref[hd]
        outs.append(acc[:MLA_V] / acc[MLA_V:MLA_V + 1])
    o_ref[0] = jnp.concatenate(outs, axis=0).T


def _mla_attn(q, k, vt, *, tq, tk, unroll):
    B, S, qw = q.shape
    vrows = vt.shape[1]
    assert S % (tk * unroll) == 0
    return pl.pallas_call(
        functools.partial(_mla_attn_kernel, tk=tk, unroll=unroll),
        grid=(B, S // tq),
        in_specs=[pl.BlockSpec((1, tq, qw), lambda b, i: (b, i, 0)),
                  pl.BlockSpec((1, S, qw), lambda b, i: (b, 0, 0)),
                  pl.BlockSpec((1, vrows, S), lambda b, i: (b, 0, 0))],
        out_specs=pl.BlockSpec((1, tq, MLA_WIDTH), lambda b, i: (b, i, 0)),
        out_shape=jax.ShapeDtypeStruct((B, S, MLA_WIDTH), F32),
        scratch_shapes=[pltpu.VMEM((MLA_HEADS, 1, tq), F32),
                        pltpu.VMEM((MLA_HEADS, MLA_VPAD, tq), F32)],
        compiler_params=_params(("parallel", "arbitrary")), name="mla_attn",
    )(q, k, vt)


def _natten_kernel(q_ref, k_ref, v_ref, bias_ref, o_ref, *, R, KH, G):
    col_head = lax.broadcasted_iota(jnp.int32, (1, NA_WIDTH), 1) // NA_HEAD_DIM
    sels = [col_head == hd for hd in range(NA_HEADS)]

    def window(i):
        r = pl.program_id(1) * G + i
        start = jnp.clip(r - KH // 2, 0, R - KH)
        return start - r + (NA_KH - 1), pl.multiple_of(start * GRID_W, GRID_W)

    def scores(i):
        dr0, off = window(i)
        q = q_ref[0, i * GRID_W:(i + 1) * GRID_W, :]
        qm = jnp.concatenate([jnp.where(sel, q, jnp.zeros_like(q)) for sel in sels], axis=0)
        kw = k_ref[0, pl.ds(off, KH * GRID_W), :]
        s = lax.dot_general(qm, kw, (((1,), (1,)), ((), ())),
                            preferred_element_type=F32)
        bias = jnp.concatenate([bias_ref[dr0 + kh] for kh in range(0, KH, 2)], axis=-1)
        logits = s + bias
        return logits, jnp.max(logits, axis=-1, keepdims=True)

    def probs(logits, m):
        p = jnp.exp(logits - m)
        return p.astype(BF16), jnp.sum(p, axis=-1, keepdims=True)

    def output(i, p, l):
        _, off = window(i)
        vw = v_ref[0, pl.ds(off, KH * GRID_W), :]
        o = jnp.dot(p, vw, preferred_element_type=F32) / l
        y = jnp.zeros((GRID_W, NA_WIDTH), F32)
        for hd, sel in enumerate(sels):
            y = y + jnp.where(sel, o[hd * GRID_W:(hd + 1) * GRID_W], 0.0)
        o_ref[0, i * GRID_W:(i + 1) * GRID_W, :] = y

    st_scores, st_probs = {}, {}
    for t in range(G + 2):
        if t < G:
            st_scores[t] = scores(t)
        if 0 <= t - 1 < G:
            st_probs[t - 1] = probs(*st_scores.pop(t - 1))
        if 0 <= t - 2 < G:
            output(t - 2, *st_probs.pop(t - 2))


def _natten(q, k, v, bias_pairs, *, G):
    B, S, _ = q.shape
    R = S // GRID_W
    KH = min(NA_KH, R)
    assert KH % 2 == 0 and R % G == 0
    return pl.pallas_call(
        functools.partial(_natten_kernel, R=R, KH=KH, G=G),
        grid=(B, R // G),
        in_specs=[pl.BlockSpec((1, G * GRID_W, NA_WIDTH), lambda b, r: (b, r, 0)),
                  pl.BlockSpec((1, S, NA_WIDTH), lambda b, r: (b, 0, 0)),
                  pl.BlockSpec((1, S, NA_WIDTH), lambda b, r: (b, 0, 0)),
                  _const_spec(bias_pairs.shape)],
        out_specs=pl.BlockSpec((1, G * GRID_W, NA_WIDTH), lambda b, r: (b, r, 0)),
        out_shape=jax.ShapeDtypeStruct((B, S, NA_WIDTH), F32),
        compiler_params=_params(("parallel", "arbitrary")), name="natten",
    )(q, k, v, bias_pairs)


def _natten_bias_pairs(rpb):
    c = np.arange(GRID_W)
    start = np.clip(c - NA_KW // 2, 0, GRID_W - NA_KW)
    v = c[None, :]
    inwin = (v >= start[:, None]) & (v < start[:, None] + NA_KW)
    dc = v - c[:, None] + (NA_KW - 1)
    onehot = (dc[:, :, None] == np.arange(2 * NA_KW - 1)[None, None, :]) & inwin[:, :, None]
    t = jnp.einsum('hdj,wvj->hdwv', rpb.astype(F32), jnp.asarray(onehot, F32),
                   precision=lax.Precision.HIGHEST)
    t = jnp.where(jnp.asarray(inwin)[None, None], t, MASK_VALUE)
    t = jnp.concatenate([t[:, :-1], t[:, 1:]], axis=-1).astype(F32)
    return t.transpose(1, 0, 2, 3).reshape(t.shape[1], NA_HEADS * GRID_W, 2 * GRID_W)


def _alt_sign_sum(x):
    rows, cols = x.shape
    if rows % F32_SUBLANES == 0:
        x = jnp.sum(x.astype(F32).reshape(rows // F32_SUBLANES, F32_SUBLANES, cols), axis=0)
    odd = (lax.broadcasted_iota(jnp.int32, x.shape, 0) & 1) == 1
    xf = x.astype(F32)
    return jnp.sum(jnp.where(odd, -xf, xf), axis=0, keepdims=True)


def _short_conv_kernel(v_ref, x1_ref, x2_ref, wv_ref, w1_ref, w2_ref, bv_ref, b1_ref, b2_ref,
                       u_ref, g1_ref, g2_ref):
    H = v_ref.shape[1]
    i = lax.broadcasted_iota(jnp.int32, v_ref.shape[1:], 0)

    def conv(x_ref, w_ref, b_ref, o_ref):
        xe = x_ref[0]
        xo = x_ref[1]
        xo_prev = jnp.where(i == 0, 0.0, pltpu.roll(xo, 1, axis=0))
        xe_next = jnp.where(i == H - 1, 0.0, pltpu.roll(xe, H - 1, axis=0))
        w = w_ref[...]
        b = b_ref[...]
        o_ref[0] = (w[0:1] * xo_prev + w[1:2] * xe + w[2:3] * xo + b).astype(o_ref.dtype)
        o_ref[1] = (w[0:1] * xe + w[1:2] * xo + w[2:3] * xe_next + b).astype(o_ref.dtype)

    conv(v_ref, wv_ref, bv_ref, u_ref)
    conv(x1_ref, w1_ref, b1_ref, g1_ref)
    conv(x2_ref, w2_ref, b2_ref, g2_ref)


def _short_conv(hy, conv_w, conv_b, B):
    _, H, _ = hy.shape
    C = HY_WIDTH
    nj = C // LANE
    per_b = 3 * nj
    seg = lambda s: pl.BlockSpec((2, H, LANE), lambda b, j: (0, 0, b * per_b + s * nj + j))
    wseg = lambda s: pl.BlockSpec((3, LANE), lambda b, j: (0, s * nj + j))
    bseg = lambda s: pl.BlockSpec((1, LANE), lambda b, j: (0, s * nj + j))
    out = pl.BlockSpec((2, H, LANE), lambda b, j: (0, 0, b * nj + j))
    return pl.pallas_call(
        _short_conv_kernel, grid=(B, nj),
        in_specs=[seg(0), seg(1), seg(2), wseg(0), wseg(1), wseg(2), bseg(0), bseg(1), bseg(2)],
        out_specs=[out, out, out],
        out_shape=[jax.ShapeDtypeStruct((2, H, B * C), BF16),
                   jax.ShapeDtypeStruct((2, H, B * C), F32),
                   jax.ShapeDtypeStruct((2, H, B * C), F32)],
        compiler_params=_params(("parallel", "parallel")), name="hy_short_conv",
    )(hy, hy, hy, conv_w, conv_w, conv_w, conv_b, conv_b, conv_b)


def _filter_kernel(z_ref, w1_ref, b1_ref, f1_ref, w2_ref, b2_ref, f2_ref, w3_ref, dec_ref,
                   hp_ref, hm_ref):
    hi = lax.Precision.HIGHEST
    dot = lambda a, b: jnp.dot(a, b, precision=hi, preferred_element_type=F32)
    h = jnp.sin(f1_ref[...] * (dot(z_ref[...], w1_ref[...]) + b1_ref[...]))
    h = jnp.sin(f2_ref[...] * (dot(h, w2_ref[...]) + b2_ref[...]))
    h = dot(h, w3_ref[...])
    dec = dec_ref[...]
    C = HY_WIDTH
    for o in range(HY_ORDER):
        hf = h[:, (2 * o) * C:(2 * o + 1) * C] * dec
        hb = h[:, (2 * o + 1) * C:(2 * o + 2) * C] * dec
        hp_ref[:, o * C:(o + 1) * C] = (hf + hb).astype(BF16)
        hm_ref[:, o * C:(o + 1) * C] = (hf - hb).astype(BF16)


def _filters(z, w1, b1, f1, w2, b2, f2, w3, decay, *, tm):
    L = z.shape[0]
    OC = HY_ORDER * HY_WIDTH
    row = lambda w: pl.BlockSpec((tm, w), lambda i: (i, 0))
    return pl.pallas_call(
        _filter_kernel, grid=(L // tm,),
        in_specs=[row(z.shape[1]), _const_spec(w1.shape), _const_spec(b1.shape),
                  _const_spec(f1.shape), _const_spec(w2.shape), _const_spec(b2.shape),
                  _const_spec(f2.shape), _const_spec(w3.shape), row(HY_WIDTH)],
        out_specs=[row(OC), row(OC)],
        out_shape=[jax.ShapeDtypeStruct((L, OC), BF16), jax.ShapeDtypeStruct((L, OC), BF16)],
        compiler_params=_params(("parallel",)), name="hy_filters",
    )(z, w1, b1, f1, w2, b2, f2, w3, decay)


def _coef_kernel(ce_ref, se_ref, co_ref, so_ref, hp_ref, hm_ref, skip_ref,
                 alo_ref, ahi_ref, blo_ref, bhi_ref, ah_ref, bh_ref, *, L):
    i = pl.program_id(0)
    tk = ce_ref.shape[0]
    inv = 1.0 / L
    skip = skip_ref[...]
    pe = jnp.dot(ce_ref[...], hp_ref[0], preferred_element_type=F32)
    po = jnp.dot(co_ref[...], hp_ref[1], preferred_element_type=F32)
    qe = jnp.dot(se_ref[...], hm_ref[0], preferred_element_type=F32)
    qo = jnp.dot(so_ref[...], hm_ref[1], preferred_element_type=F32)
    is0 = (i * tk + lax.broadcasted_iota(jnp.int32, pe.shape, 0)) == 0
    sc = jnp.where(is0, 0.5 * inv, inv)
    alo_ref[...] = sc * (pe + po + skip)
    ahi_ref[...] = sc * (pe - po + skip)
    blo_ref[...] = -inv * (qe + qo)
    bhi_ref[...] = inv * (qe - qo)

    @pl.when(i == 0)
    def _():
        gr = _alt_sign_sum(hp_ref[0]) + skip
        gi = -_alt_sign_sum(hm_ref[1])
        ah_ref[...] = jnp.broadcast_to(inv * gr, ah_ref.shape)
        bh_ref[...] = jnp.broadcast_to(inv * gi, bh_ref.shape)


def _coefs(tabs, hp, hm, skip, *, tk):
    ce, se, co, so = tabs[:4]
    H = ce.shape[0]
    OC = hp.shape[-1]
    blk = pl.BlockSpec((tk, H), lambda i: (i, 0))
    out = pl.BlockSpec((tk, OC), lambda i: (i, 0))
    half = pl.BlockSpec((8, OC), lambda i: (0, 0))
    shp = jax.ShapeDtypeStruct((H, OC), F32)
    shp_h = jax.ShapeDtypeStruct((8, OC), F32)
    return pl.pallas_call(
        functools.partial(_coef_kernel, L=2 * H), grid=(H // tk,),
        in_specs=[blk, blk, blk, blk, _const_spec(hp.shape), _const_spec(hm.shape),
                  _const_spec(skip.shape)],
        out_specs=[out, out, out, out, half, half],
        out_shape=[shp, shp, shp, shp, shp_h, shp_h],
        compiler_params=_params(("arbitrary",)), name="hy_coefs",
    )(ce, se, co, so, hp, hm, skip)


def _fwd_dft_kernel(ce_ref, se_ref, co_ref, so_ref, u_ref, alo_ref, ahi_ref, blo_ref, bhi_ref,
                    ah_ref, bh_ref, xa_ref, xb_ref, xc_ref, xd_ref, r1h_ref, r2h_ref, *, nb):
    tile = lambda x: jnp.concatenate([x] * nb, axis=-1) if nb > 1 else x
    a_lo_t, a_hi_t = tile(alo_ref[...]), tile(ahi_ref[...])
    b_lo_t, b_hi_t = tile(blo_ref[...]), tile(bhi_ref[...])
    cb = u_ref.shape[-1]
    chunk = DFT_COL_CHUNK if cb % DFT_COL_CHUNK == 0 else LANE
    for c in range(cb // chunk):
        sl = slice(c * chunk, (c + 1) * chunk)
        ue = u_ref[0, :, sl]
        uo = u_ref[1, :, sl]
        pe = jnp.dot(ce_ref[...], ue, preferred_element_type=F32)
        po = jnp.dot(co_ref[...], uo, preferred_element_type=F32)
        qe = jnp.dot(se_ref[...], ue, preferred_element_type=F32)
        qo = jnp.dot(so_ref[...], uo, preferred_element_type=F32)
        p_lo, p_hi, q_lo, q_hi = pe + po, pe - po, qe + qo, qo - qe
        a_lo, a_hi, b_lo, b_hi = a_lo_t[:, sl], a_hi_t[:, sl], b_lo_t[:, sl], b_hi_t[:, sl]
        r1_lo = p_lo * a_lo + q_lo * b_lo
        r2_lo = q_lo * a_lo - p_lo * b_lo
        r1_hi = p_hi * a_hi + q_hi * b_hi
        r2_hi = q_hi * a_hi - p_hi * b_hi
        xa_ref[:, sl] = (r1_lo + r1_hi).astype(BF16)
        xb_ref[:, sl] = (r2_lo - r2_hi).astype(BF16)
        xc_ref[:, sl] = (r1_lo - r1_hi).astype(BF16)
        xd_ref[:, sl] = (r2_lo + r2_hi).astype(BF16)

    @pl.when(pl.program_id(1) == 0)
    def _():
        ph = _alt_sign_sum(u_ref[0])
        qh = _alt_sign_sum(u_ref[1])
        a_h = tile(ah_ref[0:1])
        b_h = tile(bh_ref[0:1])
        r1h_ref[...] = jnp.broadcast_to(ph * a_h + qh * b_h, r1h_ref.shape)
        r2h_ref[...] = jnp.broadcast_to(qh * a_h - ph * b_h, r2h_ref.shape)


def _fwd_dft(tabs, u, coefs, order, *, tk, nb):
    ce, se, co, so = tabs[:4]
    alo, ahi, blo, bhi, ah, bh = coefs
    H = ce.shape[0]
    C = HY_WIDTH
    cb = nb * C
    cols = u.shape[-1]
    blk = pl.BlockSpec((tk, H), lambda c, j: (j, 0))
    coef = pl.BlockSpec((tk, C), lambda c, j: (j, order))
    half = pl.BlockSpec((8, C), lambda c, j: (0, order))
    out = pl.BlockSpec((tk, cb), lambda c, j: (j, c))
    out_h = pl.BlockSpec((8, cb), lambda c, j: (0, c))
    shp = jax.ShapeDtypeStruct((H, cols), BF16)
    shp_h = jax.ShapeDtypeStruct((8, cols), F32)
    return pl.pallas_call(
        functools.partial(_fwd_dft_kernel, nb=nb), grid=(cols // cb, H // tk),
        in_specs=[blk, blk, blk, blk, pl.BlockSpec((2, H, cb), lambda c, j: (0, 0, c)),
                  coef, coef, coef, coef, half, half],
        out_specs=[out, out, out, out, out_h, out_h],
        out_shape=[shp, shp, shp, shp, shp_h, shp_h],
        compiler_params=_params(("parallel", "arbitrary")), name="hy_fwd_dft",
    )(ce, se, co, so, u, alo, ahi, blo, bhi, ah, bh)


def _inv_dft_kernel(ce_ref, se_ref, cot_ref, sot_ref, xa_ref, xb_ref, xc_ref, xd_ref,
                    r1h_ref, r2h_ref, g_ref, z_ref):
    tm = ce_ref.shape[0]
    ye = (jnp.dot(ce_ref[...], xa_ref[...], preferred_element_type=F32)
          + jnp.dot(se_ref[...], xb_ref[...], preferred_element_type=F32))
    yo = (jnp.dot(cot_ref[...], xc_ref[...], preferred_element_type=F32)
          + jnp.dot(sot_ref[...], xd_ref[...], preferred_element_type=F32))
    rows = pl.program_id(1) * tm + lax.broadcasted_iota(jnp.int32, ye.shape, 0)
    odd = (rows & 1) == 1
    r1h = r1h_ref[0:1]
    r2h = r2h_ref[0:1]
    ye = ye + jnp.where(odd, -r1h, r1h)
    yo = yo + jnp.where(odd, -r2h, r2h)
    z_ref[0] = (g_ref[0] * ye).astype(z_ref.dtype)
    z_ref[1] = (g_ref[1] * yo).astype(z_ref.dtype)


def _inv_dft(tabs, xs, gate, out_dtype, *, tm, nb):
    ce, se, _, _, cot, sot = tabs
    xa, xb, xc, xd, r1h, r2h = xs
    H = ce.shape[0]
    cb = nb * HY_WIDTH
    cols = xa.shape[-1]
    blk = pl.BlockSpec((tm, H), lambda c, i: (i, 0))
    full = pl.BlockSpec((H, cb), lambda c, i: (0, c))
    half = pl.BlockSpec((8, cb), lambda c, i: (0, c))
    tile = pl.BlockSpec((2, tm, cb), lambda c, i: (0, i, c))
    return pl.pallas_call(
        _inv_dft_kernel, grid=(cols // cb, H // tm),
        in_specs=[blk, blk, blk, blk, full, full, full, full, half, half, tile],
        out_specs=tile, out_shape=jax.ShapeDtypeStruct((2, H, cols), out_dtype),
        compiler_params=_params(("parallel", "arbitrary")), name="hy_inv_dft",
    )(ce, se, cot, sot, xa, xb, xc, xd, r1h, r2h, gate)


def _hyena_features(L):
    t_idx = jnp.arange(L, dtype=F32)[:, None]
    t_norm = jnp.linspace(0.0, 1.0, L, dtype=F32)[:, None]
    bands = jnp.linspace(1e-4, HY_BANDS - 1, HY_BANDS, dtype=F32)[None, :]
    ang = 2.0 * math.pi * t_idx * bands / L
    z = jnp.concatenate([t_norm, jnp.cos(ang), jnp.sin(ang)], axis=-1)
    z = jnp.pad(z, ((0, 0), (0, LANE - HY_EMB)))
    deltas = jnp.linspace(math.log(HY_TARGET) / HY_SLOW_DECAY,
                          math.log(HY_TARGET) / HY_FAST_DECAY, HY_WIDTH, dtype=F32)
    decay = jnp.exp(-t_norm * jnp.abs(deltas)[None, :])
    deint = lambda a: jnp.concatenate([a[0::2], a[1::2]], axis=0)
    return deint(z), deint(decay)


def _mix_ffn_kernel(x_ref, ya_ref, yb_ref, yc_ref, gmix_ref, wout_ref, g2_ref, wg_ref, wu_ref,
                    wd_ref, gf_ref, o_ref, yb_scr, *, n_chunks, final_norm):
    gmix = gmix_ref[...]
    ca = ya_ref.shape[-1]
    cb = yb_ref.shape[-1]
    half = yb_ref.shape[1]
    for j in range(cb // LANE):
        sl = slice(j * LANE, (j + 1) * LANE)
        yb_scr[j, pl.ds(0, half, stride=2), :] = yb_ref[0, :, sl]
        yb_scr[j, pl.ds(1, half, stride=2), :] = yb_ref[1, :, sl]
    yb = jnp.concatenate([yb_scr[j] for j in range(cb // LANE)], axis=-1)
    ymix = jnp.concatenate([
        _rms(ya_ref[0], gmix[:, :ca]),
        _rms(yb, gmix[:, ca:ca + cb]),
        _rms(yc_ref[0], gmix[:, ca + cb:]),
    ], axis=-1).astype(BF16)
    x = x_ref[0] + jnp.dot(ymix, wout_ref[...], preferred_element_type=F32)
    h2 = _rms(x, g2_ref[...]).astype(BF16)
    ff = wg_ref.shape[1]
    ch = ff // n_chunks
    acc = x
    for c in range(n_chunks):
        sl = slice(c * ch, (c + 1) * ch)
        gate = jnp.dot(h2, wg_ref[:, sl], preferred_element_type=F32)
        up = jnp.dot(h2, wu_ref[:, sl], preferred_element_type=F32)
        act = (gate * jax.nn.sigmoid(gate) * up).astype(BF16)
        acc = acc + jnp.dot(act, wd_ref[sl, :], preferred_element_type=F32)
    if final_norm:
        acc = _rms(acc, gf_ref[...])
    o_ref[0] = acc


def _mix_ffn(x, ya, yb, yc, gmix, wout, g2, wg, wu, wd, gf, *, tm, final_norm):
    B, S, D = x.shape
    ns = S // tm
    tok = lambda w: pl.BlockSpec((1, tm, w), lambda i: (i // ns, i % ns, 0))
    once = lambda a: pl.BlockSpec(a.shape, lambda i: (0,) * a.ndim,
                                  pipeline_mode=pl.Buffered(1))
    n_chunks = 2 if (wg.shape[1] // 2) % LANE == 0 else 1
    return pl.pallas_call(
        functools.partial(_mix_ffn_kernel, n_chunks=n_chunks, final_norm=final_norm),
        grid=(B * ns,),
        in_specs=[tok(D), tok(ya.shape[-1]),
                  pl.BlockSpec((2, tm // 2, HY_WIDTH), lambda i: (0, i % ns, i // ns)),
                  tok(yc.shape[-1]), once(gmix), once(wout), once(g2), once(wg), once(wu),
                  once(wd), once(gf)],
        out_specs=tok(D), out_shape=jax.ShapeDtypeStruct((B, S, D), F32),
        scratch_shapes=[pltpu.VMEM((HY_WIDTH // LANE, tm, LANE), F32)],
        compiler_params=_params(("parallel",)), name="mix_ffn",
    )(x, ya, yb, yc, gmix, wout, g2, wg, wu, wd, gf)


def _rope_tables(S):
    half = MLA_ROPE // 2
    pos = jnp.arange(S, dtype=F32)
    inv = ROPE_THETA ** (-jnp.arange(0, MLA_ROPE, 2, dtype=F32) / MLA_ROPE)
    ang = pos[:, None] * inv[None, :]
    cos, sin = jnp.cos(ang), jnp.sin(ang)
    z_lo = jnp.zeros((S, MLA_NOPE), F32)
    z_hi = jnp.zeros((S, HEAD_PAD - MLA_NOPE - 2 * half), F32)
    ck = jnp.concatenate([z_lo, cos, cos, z_hi], axis=-1)
    sk = jnp.concatenate([z_lo, -sin, sin, z_hi], axis=-1)
    scale = math.log2(math.e) / math.sqrt(MLA_NOPE + MLA_ROPE)
    cq = jnp.concatenate([jnp.ones((S, MLA_NOPE), F32), cos, cos, z_hi], axis=-1) * scale
    sq = sk * scale
    return cq, sq, ck, sk


def _pad_heads(w, n_heads, width, keep):
    K = w.shape[0]
    w = w.reshape(K, n_heads, width)[:, :, :keep]
    w = jnp.pad(w, ((0, 0), (0, 0), (0, HEAD_PAD - keep)))
    return w.reshape(K, n_heads * HEAD_PAD)


def _swap_rope_cols(w, n_heads, width):
    K = w.shape[0]
    half = MLA_ROPE // 2
    w = w.reshape(K, n_heads, width)
    a = w[:, :, MLA_NOPE:MLA_NOPE + half]
    b = w[:, :, MLA_NOPE + half:MLA_NOPE + 2 * half]
    out = jnp.concatenate([jnp.zeros_like(w[:, :, :MLA_NOPE]), b, a], axis=-1)
    return out.reshape(K, n_heads * width)


def _layer_weights(w_in, w_uq, w_ukv):
    D = w_in.shape[0]
    o = MLA_Q_RANK + MLA_KV_RANK
    w_kpe = w_in[:, o:o + MLA_ROPE]
    zl = jnp.zeros((D, MLA_NOPE), F32)
    zh = jnp.zeros((D, HEAD_PAD - MLA_NOPE - MLA_ROPE), F32)
    half = MLA_ROPE // 2
    kpe_pad = jnp.concatenate([zl, w_kpe, zh], axis=-1)
    kpe_sw = jnp.concatenate([zl, w_kpe[:, half:], w_kpe[:, :half], zh], axis=-1)
    wlat = jnp.concatenate([w_in[:, :o], kpe_pad, kpe_sw], axis=-1).astype(BF16)
    o2 = o + MLA_ROPE
    why = w_in[:, o2:o2 + 3 * HY_WIDTH].astype(BF16)
    wna = w_in[:, o2 + 3 * HY_WIDTH:].astype(BF16)
    qd = MLA_NOPE + MLA_ROPE
    wq = _pad_heads(w_uq, MLA_HEADS, qd, qd).astype(BF16)
    wqs = _pad_heads(_swap_rope_cols(w_uq, MLA_HEADS, qd), MLA_HEADS, qd, qd).astype(BF16)
    kvd = MLA_NOPE + MLA_V
    wk = _pad_heads(w_ukv, MLA_HEADS, kvd, MLA_NOPE).astype(BF16)
    wv = w_ukv.reshape(-1, MLA_HEADS, kvd)[:, :, MLA_NOPE:].reshape(-1, MLA_WIDTH)
    wvt = wv.T.astype(BF16)
    return wlat, why, wna, wq, wqs, wk, wvt


def _pad2(a, rows, cols):
    return jnp.pad(a, ((0, rows - a.shape[0]), (0, cols - a.shape[1])))


def kernel(x, norm1_g, w_in, mla_q_norm_g, mla_w_uq, mla_kv_norm_g, mla_w_ukv, hy_conv_w, hy_conv_b, hy_filt_w1, hy_filt_b1, hy_filt_freq1, hy_filt_w2, hy_filt_b2, hy_filt_freq2, hy_filt_w3, hy_skip, na_rpb, mix_norm_g, w_out, norm2_g, ffn_w_gate, ffn_w_up, ffn_w_down, final_norm_g):
    B, S, D = x.shape
    depth = w_in.shape[0]
    L = S
    tm = min(512, S)
    tq = min(512, S)
    tkv = min(256, S)
    kv_unroll = max(1, min(8, S // tkv))
    t_dft = min(256, L // 2)
    nb = 2 if B % 2 == 0 else 1

    cq, sq, ck, sk = _rope_tables(S)
    tabs = _dft_tables(L)
    z_feat, decay = _hyena_features(L)
    row = lambda v: v.reshape(1, -1)

    for l in range(depth):
        wlat, why, wna, wq, wqs, wk, wvt = _layer_weights(w_in[l], mla_w_uq[l], mla_w_ukv[l])
        q, k, vt, hy, naq, nak, nav = _inproj(
            x, row(norm1_g[l]), wlat, why, wna, row(mla_q_norm_g[l]), wq, wqs,
            row(mla_kv_norm_g[l]), wk, wvt, cq, sq, ck, sk, tm=tm)

        y_a = _mla_attn(q, k, vt, tq=tq, tk=tkv, unroll=kv_unroll)
        y_c = _natten(naq, nak, nav, _natten_bias_pairs(na_rpb[l]), G=min(8, S // GRID_W))

        u, g1, g2 = _short_conv(hy, hy_conv_w[l], row(hy_conv_b[l]), B)
        hp, hm = _filters(
            z_feat, _pad2(hy_filt_w1[l], LANE, LANE), _pad2(row(hy_filt_b1[l]), 1, LANE),
            _pad2(row(hy_filt_freq1[l]), 1, LANE), _pad2(hy_filt_w2[l], LANE, LANE),
            _pad2(row(hy_filt_b2[l]), 1, LANE), _pad2(row(hy_filt_freq2[l]), 1, LANE),
            _pad2(hy_filt_w3[l], LANE, hy_filt_w3.shape[-1]), decay, tm=tm)
        halves = lambda a: a.reshape(2, L // 2, a.shape[-1])
        coefs = _coefs(tabs, halves(hp), halves(hm), row(hy_skip[l]), tk=t_dft)
        xs = _fwd_dft(tabs, u, coefs, 0, tk=t_dft, nb=nb)
        z1 = _inv_dft(tabs, xs, g1, BF16, tm=t_dft, nb=nb)
        xs = _fwd_dft(tabs, z1, coefs, 1, tk=t_dft, nb=nb)
        y_b = _inv_dft(tabs, xs, g2, F32, tm=t_dft, nb=nb)

        x = _mix_ffn(x, y_a, y_b, y_c, row(mix_norm_g[l]), w_out[l].astype(BF16),
                     row(norm2_g[l]), ffn_w_gate[l].astype(BF16), ffn_w_up[l].astype(BF16),
                     ffn_w_down[l].astype(BF16), row(final_norm_g),
                     tm=tm, final_norm=(l == depth - 1))
    return x
```

```python
import functools
import math

import jax
import jax.numpy as jnp
import numpy as np
from jax import lax
from jax.experimental import pallas as pl
from jax.experimental.pallas import tpu as pltpu

F32 = jnp.float32
BF16 = jnp.bfloat16

NORM_EPS = 1e-6
MLA_HEADS = 6
MLA_NOPE = 64
MLA_ROPE = 32
MLA_V = 64
MLA_Q_RANK = 256
MLA_KV_RANK = 128
ROPE_THETA = 10000.0
HY_WIDTH = 384
HY_ORDER = 2
HY_BANDS = 8
HY_EMB = 1 + 2 * HY_BANDS
HY_FFN = 64
HY_FAST_DECAY = 0.3
HY_SLOW_DECAY = 1.5
HY_TARGET = 1e-2
NA_HEADS = 4
NA_HEAD_DIM = 64
NA_WIDTH = NA_HEADS * NA_HEAD_DIM
GRID_W = 64
NA_KH = 8
NA_KW = 16
MLA_WIDTH = MLA_HEADS * MLA_V

LANE = 128
HEAD_PAD = 128
F32_SUBLANES = 8
BF16_SUBLANES = 16
MLA_VPAD = MLA_V + BF16_SUBLANES
VMEM_LIMIT = 56 * 1024 * 1024
MASK_VALUE = -1e30
DFT_COL_CHUNK = 256


def _params(sem, vmem=VMEM_LIMIT):
    return pltpu.CompilerParams(dimension_semantics=sem, vmem_limit_bytes=vmem)


def _rms(x, g):
    return x * lax.rsqrt(jnp.mean(x * x, axis=-1, keepdims=True) + NORM_EPS) * g


def _const_spec(shape):
    nd = len(shape)
    return pl.BlockSpec(shape, lambda *_: (0,) * nd)


def _dft_table_kernel(ce_ref, se_ref, co_ref, so_ref, cot_ref, sot_ref,
                      eec_ref, ees_ref, eoc_ref, eos_ref, *, L, tk):
    i = pl.program_id(0)
    n = 2 * L
    H = L // 2
    w = 2.0 * math.pi / n

    def angle(prod):
        return (prod & (n - 1)).astype(F32) * w

    @pl.when(i == 0)
    def _():
        r = lax.broadcasted_iota(jnp.int32, (tk, H), 0)
        c = lax.broadcasted_iota(jnp.int32, (tk, H), 1)
        ang_e = angle(r * (2 * c))
        ang_o = angle(r * (2 * c + 1))
        eec_ref[...] = jnp.cos(ang_e)
        ees_ref[...] = jnp.sin(ang_e)
        eoc_ref[...] = jnp.cos(ang_o)
        eos_ref[...] = jnp.sin(ang_o)

    k0 = i * tk
    c1 = lax.broadcasted_iota(jnp.int32, (1, H), 1)

    def rotate(a0, ec_ref, es_ref):
        ca, sa = jnp.cos(a0), jnp.sin(a0)
        ec, es = ec_ref[...], es_ref[...]
        return (ca * ec - sa * es).astype(BF16), (sa * ec + ca * es).astype(BF16)

    ce_ref[...], se_ref[...] = rotate(angle(k0 * (2 * c1)), eec_ref, ees_ref)
    co_ref[...], so_ref[...] = rotate(angle(k0 * (2 * c1 + 1)), eoc_ref, eos_ref)
    cot_ref[...], sot_ref[...] = rotate(angle(c1 * (2 * k0 + 1)), eec_ref, ees_ref)


def _dft_tables(L):
    assert L & (L - 1) == 0, "sequence length must be a power of two"
    H = L // 2
    tk = min(128, H)
    spec = pl.BlockSpec((tk, H), lambda i: (i, 0))
    shp = jax.ShapeDtypeStruct((H, H), BF16)
    return pl.pallas_call(
        functools.partial(_dft_table_kernel, L=L, tk=tk),
        grid=(H // tk,),
        out_specs=[spec] * 6,
        out_shape=[shp] * 6,
        scratch_shapes=[pltpu.VMEM((tk, H), F32)] * 4,
        compiler_params=_params(("arbitrary",)),
        name="dft_tables",
    )()


def _swap_rope_halves(x):
    half = MLA_ROPE // 2
    lane = lax.broadcasted_iota(jnp.int32, (1, x.shape[1]), 1) % HEAD_PAD
    first = (lane >= MLA_NOPE) & (lane < MLA_NOPE + half)
    width = x.shape[1]
    return jnp.where(first, pltpu.roll(x, width - half, axis=1), pltpu.roll(x, half, axis=1))


def _inproj_kernel(x_ref, g1_ref, wlat_ref, why_ref, wna_ref, gq_ref, wq_ref,
                   gkv_ref, wk_ref, wvt_ref, cq_ref, sq_ref, ck_ref, sk_ref,
                   q_ref, k_ref, vt_ref, hy_ref, naq_ref, nak_ref, nav_ref, hy_scr):
    x = x_ref[0]
    h = _rms(x, g1_ref[...]).astype(BF16)
    lat = jnp.dot(h, wlat_ref[...], preferred_element_type=F32)
    hy = jnp.dot(h, why_ref[...], preferred_element_type=F32)
    half = hy.shape[0] // 2
    for j in range(hy.shape[1] // LANE):
        sl = slice(j * LANE, (j + 1) * LANE)
        hy_scr[j] = hy[:, sl]
        hy_ref[0, :, sl] = hy_scr[j, pl.ds(0, half, stride=2), :].astype(BF16)
        hy_ref[1, :, sl] = hy_scr[j, pl.ds(1, half, stride=2), :].astype(BF16)
    na = jnp.dot(h, wna_ref[...], preferred_element_type=F32)
    naq_ref[0] = (na[:, :NA_WIDTH] * (1.0 / math.sqrt(NA_HEAD_DIM))).astype(BF16)
    nak_ref[0] = na[:, NA_WIDTH:2 * NA_WIDTH].astype(BF16)
    nav_ref[0] = na[:, 2 * NA_WIDTH:].astype(BF16)

    c_q = lat[:, :MLA_Q_RANK]
    c_kv = lat[:, MLA_Q_RANK:MLA_Q_RANK + MLA_KV_RANK]
    o = MLA_Q_RANK + MLA_KV_RANK
    kpe = lat[:, o:o + HEAD_PAD]

    cqn = _rms(c_q, gq_ref[...]).astype(BF16)
    qf = jnp.dot(cqn, wq_ref[...], preferred_element_type=F32)
    qs = _swap_rope_halves(qf)
    cq = cq_ref[...]
    sq = sq_ref[...]
    ckvn = _rms(c_kv, gkv_ref[...])
    kf = jnp.dot(ckvn.astype(BF16), wk_ref[...], preferred_element_type=F32)
    kpe_r = kpe * ck_ref[...] + _swap_rope_halves(kpe) * sk_ref[...]
    for hd in range(MLA_HEADS):
        sl = slice(hd * HEAD_PAD, (hd + 1) * HEAD_PAD)
        q_ref[0, :, sl] = (qf[:, sl] * cq + qs[:, sl] * sq).astype(BF16)
        k_ref[0, :, sl] = (kf[:, sl] + kpe_r).astype(BF16)
    vt = jnp.dot(wvt_ref[...], ckvn.T.astype(BF16), preferred_element_type=F32).astype(BF16)
    ones = jnp.ones((MLA_VPAD - MLA_V, vt.shape[1]), BF16)
    for hd in range(MLA_HEADS):
        vt_ref[0, hd * MLA_VPAD:hd * MLA_VPAD + MLA_V, :] = vt[hd * MLA_V:(hd + 1) * MLA_V]
        vt_ref[0, hd * MLA_VPAD + MLA_V:(hd + 1) * MLA_VPAD, :] = ones


def _inproj(x, g1, wlat, why, wna, gq, wq, gkv, wk, wvt, cq, sq, ck, sk, *, tm):
    B, S, D = x.shape
    ns = S // tm
    hyw = why.shape[1]
    qw = MLA_HEADS * HEAD_PAD
    tok = lambda w: pl.BlockSpec((1, tm, w), lambda i: (i // ns, i % ns, 0))
    tab = pl.BlockSpec((tm, HEAD_PAD), lambda i: (i % ns, 0))
    in_specs = [tok(D), _const_spec(g1.shape), _const_spec(wlat.shape), _const_spec(why.shape),
                _const_spec(wna.shape), _const_spec(gq.shape), _const_spec(wq.shape),
                _const_spec(gkv.shape), _const_spec(wk.shape),
                _const_spec(wvt.shape), tab, tab, tab, tab]
    out_specs = [tok(qw), tok(qw),
                 pl.BlockSpec((1, MLA_HEADS * MLA_VPAD, tm), lambda i: (i // ns, 0, i % ns)),
                 pl.BlockSpec((2, tm // 2, hyw), lambda i: (0, i % ns, i // ns)),
                 tok(NA_WIDTH), tok(NA_WIDTH), tok(NA_WIDTH)]
    out_shape = [jax.ShapeDtypeStruct((B, S, qw), BF16),
                 jax.ShapeDtypeStruct((B, S, qw), BF16),
                 jax.ShapeDtypeStruct((B, MLA_HEADS * MLA_VPAD, S), BF16),
                 jax.ShapeDtypeStruct((2, S // 2, B * hyw), BF16),
                 jax.ShapeDtypeStruct((B, S, NA_WIDTH), BF16),
                 jax.ShapeDtypeStruct((B, S, NA_WIDTH), BF16),
                 jax.ShapeDtypeStruct((B, S, NA_WIDTH), BF16)]
    return pl.pallas_call(
        _inproj_kernel, grid=(B * ns,), in_specs=in_specs, out_specs=out_specs,
        out_shape=out_shape, scratch_shapes=[pltpu.VMEM((hyw // LANE, tm, LANE), F32)],
        compiler_params=_params(("parallel",)), name="inproj",
    )(x, g1, wlat, why, wna, gq, wq, gkv, wk, wvt, cq, sq, ck, sk)


def _mla_attn_kernel(q_ref, k_ref, vt_ref, o_ref, m_ref, acc_ref, *, tk, unroll):
    S = k_ref.shape[1]
    nk = S // tk
    m_ref[...] = jnp.full(m_ref.shape, -jnp.inf, F32)
    acc_ref[...] = jnp.zeros(acc_ref.shape, F32)

    def body(j, carry):
        base = j * (unroll * tk)

        def scores(c, hd):
            off = pl.multiple_of(base + c * tk, tk)
            sl = slice(hd * HEAD_PAD, (hd + 1) * HEAD_PAD)
            kj = k_ref[0, pl.ds(off, tk), sl]
            s = lax.dot_general(kj, q_ref[0, :, sl], (((1,), (1,)), ((), ())),
                                preferred_element_type=F32)
            m_old = m_ref[hd]
            m_new = jnp.maximum(m_old, jnp.max(s, axis=0, keepdims=True))
            m_ref[hd] = m_new
            return s, m_old, m_new

        def probs(s, m_old, m_new):
            return jnp.exp2(s - m_new).astype(BF16), jnp.exp2(m_old - m_new)

        def accumulate(c, hd, p, alpha):
            off = pl.multiple_of(base + c * tk, tk)
            vj = vt_ref[0, hd * MLA_VPAD:(hd + 1) * MLA_VPAD, pl.ds(off, tk)]
            acc_ref[hd] = alpha * acc_ref[hd] + jnp.dot(vj, p, preferred_element_type=F32)

        items = [(c, hd) for c in range(unroll) for hd in range(MLA_HEADS)]
        n = len(items)
        st_scores, st_probs = {}, {}
        for t in range(n + 2):
            if t < n:
                st_scores[t] = scores(*items[t])
            if 0 <= t - 1 < n:
                st_probs[t - 1] = probs(*st_scores.pop(t - 1))
            if 0 <= t - 2 < n:
                accumulate(*items[t - 2], *st_probs.pop(t - 2))
        return carry

    lax.fori_loop(0, nk // unroll, body, 0)
    outs = []
    for hd in range(MLA_HEADS):
        acc = acc_ref[hd]
        outs.append(acc[:MLA_V] / acc[MLA_V:MLA_V + 1])
    o_ref[0] = jnp.concatenate(outs, axis=0).T.astype(o_ref.dtype)


def _mla_attn(q, k, vt, *, tq, tk, unroll):
    B, S, qw = q.shape
    vrows = vt.shape[1]
    assert S % (tk * unroll) == 0
    return pl.pallas_call(
        functools.partial(_mla_attn_kernel, tk=tk, unroll=unroll),
        grid=(B, S // tq),
        in_specs=[pl.BlockSpec((1, tq, qw), lambda b, i: (b, i, 0)),
                  pl.BlockSpec((1, S, qw), lambda b, i: (b, 0, 0)),
                  pl.BlockSpec((1, vrows, S), lambda b, i: (b, 0, 0))],
        out_specs=pl.BlockSpec((1, tq, MLA_WIDTH), lambda b, i: (b, i, 0)),
        out_shape=jax.ShapeDtypeStruct((B, S, MLA_WIDTH), BF16),
        scratch_shapes=[pltpu.VMEM((MLA_HEADS, 1, tq), F32),
                        pltpu.VMEM((MLA_HEADS, MLA_VPAD, tq), F32)],
        compiler_params=_params(("parallel", "arbitrary")), name="mla_attn",
    )(q, k, vt)


def _natten_kernel(q_ref, k_ref, v_ref, bias_ref, o_ref, *, R, KH, G):
    col_head = lax.broadcasted_iota(jnp.int32, (1, NA_WIDTH), 1) // NA_HEAD_DIM
    sels = [col_head == hd for hd in range(NA_HEADS)]

    def window(i):
        r = pl.program_id(1) * G + i
        start = jnp.clip(r - KH // 2, 0, R - KH)
        return start - r + (NA_KH - 1), pl.multiple_of(start * GRID_W, GRID_W)

    def scores(i):
        dr0, off = window(i)
        q = q_ref[0, i * GRID_W:(i + 1) * GRID_W, :]
        qm = jnp.concatenate([jnp.where(sel, q, jnp.zeros_like(q)) for sel in sels], axis=0)
        kw = k_ref[0, pl.ds(off, KH * GRID_W), :]
        s = lax.dot_general(qm, kw, (((1,), (1,)), ((), ())),
                            preferred_element_type=F32)
        bias = jnp.concatenate([bias_ref[dr0 + kh] for kh in range(0, KH, 2)], axis=-1)
        logits = s + bias
        return logits, jnp.max(logits, axis=-1, keepdims=True)

    def probs(logits, m):
        p = jnp.exp(logits - m)
        return p.astype(BF16), jnp.sum(p, axis=-1, keepdims=True)

    def output(i, p, l):
        _, off = window(i)
        vw = v_ref[0, pl.ds(off, KH * GRID_W), :]
        o = jnp.dot(p, vw, preferred_element_type=F32) / l
        y = jnp.zeros((GRID_W, NA_WIDTH), F32)
        for hd, sel in enumerate(sels):
            y = y + jnp.where(sel, o[hd * GRID_W:(hd + 1) * GRID_W], 0.0)
        o_ref[0, i * GRID_W:(i + 1) * GRID_W, :] = y.astype(o_ref.dtype)

    st_scores, st_probs = {}, {}
    for t in range(G + 2):
        if t < G:
            st_scores[t] = scores(t)
        if 0 <= t - 1 < G:
            st_probs[t - 1] = probs(*st_scores.pop(t - 1))
        if 0 <= t - 2 < G:
            output(t - 2, *st_probs.pop(t - 2))


def _natten(q, k, v, bias_pairs, *, G):
    B, S, _ = q.shape
    R = S // GRID_W
    KH = min(NA_KH, R)
    assert KH % 2 == 0 and R % G == 0
    return pl.pallas_call(
        functools.partial(_natten_kernel, R=R, KH=KH, G=G),
        grid=(B, R // G),
        in_specs=[pl.BlockSpec((1, G * GRID_W, NA_WIDTH), lambda b, r: (b, r, 0)),
                  pl.BlockSpec((1, S, NA_WIDTH), lambda b, r: (b, 0, 0)),
                  pl.BlockSpec((1, S, NA_WIDTH), lambda b, r: (b, 0, 0)),
                  _const_spec(bias_pairs.shape)],
        out_specs=pl.BlockSpec((1, G * GRID_W, NA_WIDTH), lambda b, r: (b, r, 0)),
        out_shape=jax.ShapeDtypeStruct((B, S, NA_WIDTH), BF16),
        compiler_params=_params(("parallel", "arbitrary")), name="natten",
    )(q, k, v, bias_pairs)


def _natten_bias_pairs(rpb):
    c = np.arange(GRID_W)
    start = np.clip(c - NA_KW // 2, 0, GRID_W - NA_KW)
    v = c[None, :]
    inwin = (v >= start[:, None]) & (v < start[:, None] + NA_KW)
    dc = v - c[:, None] + (NA_KW - 1)
    onehot = (dc[:, :, None] == np.arange(2 * NA_KW - 1)[None, None, :]) & inwin[:, :, None]
    t = jnp.einsum('hdj,wvj->hdwv', rpb.astype(F32), jnp.asarray(onehot, F32),
                   precision=lax.Precision.HIGHEST)
    t = jnp.where(jnp.asarray(inwin)[None, None], t, MASK_VALUE)
    t = jnp.concatenate([t[:, :-1], t[:, 1:]], axis=-1).astype(F32)
    return t.transpose(1, 0, 2, 3).reshape(t.shape[1], NA_HEADS * GRID_W, 2 * GRID_W)


def _alt_sign_sum(x):
    rows, cols = x.shape
    if rows % F32_SUBLANES == 0:
        x = jnp.sum(x.astype(F32).reshape(rows // F32_SUBLANES, F32_SUBLANES, cols), axis=0)
    odd = (lax.broadcasted_iota(jnp.int32, x.shape, 0) & 1) == 1
    xf = x.astype(F32)
    return jnp.sum(jnp.where(odd, -xf, xf), axis=0, keepdims=True)


def _short_conv_kernel(v_ref, x1_ref, x2_ref, wv_ref, w1_ref, w2_ref, bv_ref, b1_ref, b2_ref,
                       u_ref, g1_ref, g2_ref):
    H = v_ref.shape[1]
    i = lax.broadcasted_iota(jnp.int32, v_ref.shape[1:], 0)

    def conv(x_ref, w_ref, b_ref, o_ref):
        xe = x_ref[0].astype(F32)
        xo = x_ref[1].astype(F32)
        xo_prev = jnp.where(i == 0, 0.0, pltpu.roll(xo, 1, axis=0))
        xe_next = jnp.where(i == H - 1, 0.0, pltpu.roll(xe, H - 1, axis=0))
        w = w_ref[...]
        b = b_ref[...]
        o_ref[0] = (w[0:1] * xo_prev + w[1:2] * xe + w[2:3] * xo + b).astype(o_ref.dtype)
        o_ref[1] = (w[0:1] * xe + w[1:2] * xo + w[2:3] * xe_next + b).astype(o_ref.dtype)

    conv(v_ref, wv_ref, bv_ref, u_ref)
    conv(x1_ref, w1_ref, b1_ref, g1_ref)
    conv(x2_ref, w2_ref, b2_ref, g2_ref)


def _short_conv(hy, conv_w, conv_b, B):
    _, H, _ = hy.shape
    C = HY_WIDTH
    nj = C // LANE
    per_b = 3 * nj
    seg = lambda s: pl.BlockSpec((2, H, LANE), lambda b, j: (0, 0, b * per_b + s * nj + j))
    wseg = lambda s: pl.BlockSpec((3, LANE), lambda b, j: (0, s * nj + j))
    bseg = lambda s: pl.BlockSpec((1, LANE), lambda b, j: (0, s * nj + j))
    out = pl.BlockSpec((2, H, LANE), lambda b, j: (0, 0, b * nj + j))
    return pl.pallas_call(
        _short_conv_kernel, grid=(B, nj),
        in_specs=[seg(0), seg(1), seg(2), wseg(0), wseg(1), wseg(2), bseg(0), bseg(1), bseg(2)],
        out_specs=[out, out, out],
        out_shape=[jax.ShapeDtypeStruct((2, H, B * C), BF16)] * 3,
        compiler_params=_params(("parallel", "parallel")), name="hy_short_conv",
    )(hy, hy, hy, conv_w, conv_w, conv_w, conv_b, conv_b, conv_b)


def _filter_kernel(z_ref, w1_ref, b1_ref, f1_ref, w2_ref, b2_ref, f2_ref, w3_ref, dec_ref,
                   hp_ref, hm_ref):
    hi = lax.Precision.HIGHEST
    dot = lambda a, b: jnp.dot(a, b, precision=hi, preferred_element_type=F32)
    h = jnp.sin(f1_ref[...] * (dot(z_ref[...], w1_ref[...]) + b1_ref[...]))
    h = jnp.sin(f2_ref[...] * (dot(h, w2_ref[...]) + b2_ref[...]))
    h = jnp.dot(h.astype(BF16), w3_ref[...].astype(BF16),
                preferred_element_type=F32)
    dec = dec_ref[...]
    C = HY_WIDTH
    for o in range(HY_ORDER):
        hf = h[:, (2 * o) * C:(2 * o + 1) * C] * dec
        hb = h[:, (2 * o + 1) * C:(2 * o + 2) * C] * dec
        hp_ref[:, o * C:(o + 1) * C] = (hf + hb).astype(BF16)
        hm_ref[:, o * C:(o + 1) * C] = (hf - hb).astype(BF16)


def _filters(z, w1, b1, f1, w2, b2, f2, w3, decay, *, tm):
    L = z.shape[0]
    OC = HY_ORDER * HY_WIDTH
    row = lambda w: pl.BlockSpec((tm, w), lambda i: (i, 0))
    return pl.pallas_call(
        _filter_kernel, grid=(L // tm,),
        in_specs=[row(z.shape[1]), _const_spec(w1.shape), _const_spec(b1.shape),
                  _const_spec(f1.shape), _const_spec(w2.shape), _const_spec(b2.shape),
                  _const_spec(f2.shape), _const_spec(w3.shape), row(HY_WIDTH)],
        out_specs=[row(OC), row(OC)],
        out_shape=[jax.ShapeDtypeStruct((L, OC), BF16), jax.ShapeDtypeStruct((L, OC), BF16)],
        compiler_params=_params(("parallel",)), name="hy_filters",
    )(z, w1, b1, f1, w2, b2, f2, w3, decay)


def _coef_kernel(ce_ref, se_ref, co_ref, so_ref, hp_ref, hm_ref, skip_ref,
                 alo_ref, ahi_ref, blo_ref, bhi_ref, ah_ref, bh_ref, *, L):
    i = pl.program_id(0)
    tk = ce_ref.shape[0]
    inv = 1.0 / L
    skip = skip_ref[...]
    pe = jnp.dot(ce_ref[...], hp_ref[0], preferred_element_type=F32)
    po = jnp.dot(co_ref[...], hp_ref[1], preferred_element_type=F32)
    qe = jnp.dot(se_ref[...], hm_ref[0], preferred_element_type=F32)
    qo = jnp.dot(so_ref[...], hm_ref[1], preferred_element_type=F32)
    is0 = (i * tk + lax.broadcasted_iota(jnp.int32, pe.shape, 0)) == 0
    sc = jnp.where(is0, 0.5 * inv, inv)
    alo_ref[...] = sc * (pe + po + skip)
    ahi_ref[...] = sc * (pe - po + skip)
    blo_ref[...] = -inv * (qe + qo)
    bhi_ref[...] = inv * (qe - qo)

    @pl.when(i == 0)
    def _():
        gr = _alt_sign_sum(hp_ref[0]) + skip
        gi = -_alt_sign_sum(hm_ref[1])
        ah_ref[...] = jnp.broadcast_to(inv * gr, ah_ref.shape)
        bh_ref[...] = jnp.broadcast_to(inv * gi, bh_ref.shape)


def _coefs(tabs, hp, hm, skip, *, tk):
    ce, se, co, so = tabs[:4]
    H = ce.shape[0]
    OC = hp.shape[-1]
    blk = pl.BlockSpec((tk, H), lambda i: (i, 0))
    out = pl.BlockSpec((tk, OC), lambda i: (i, 0))
    half = pl.BlockSpec((8, OC), lambda i: (0, 0))
    shp = jax.ShapeDtypeStruct((H, OC), F32)
    shp_h = jax.ShapeDtypeStruct((8, OC), F32)
    return pl.pallas_call(
        functools.partial(_coef_kernel, L=2 * H), grid=(H // tk,),
        in_specs=[blk, blk, blk, blk, _const_spec(hp.shape), _const_spec(hm.shape),
                  _const_spec(skip.shape)],
        out_specs=[out, out, out, out, half, half],
        out_shape=[shp, shp, shp, shp, shp_h, shp_h],
        compiler_params=_params(("arbitrary",)), name="hy_coefs",
    )(ce, se, co, so, hp, hm, skip)


def _fwd_dft_kernel(ce_ref, se_ref, co_ref, so_ref, u_ref, alo_ref, ahi_ref, blo_ref, bhi_ref,
                    ah_ref, bh_ref, xa_ref, xb_ref, xc_ref, xd_ref, r1h_ref, r2h_ref, *, nb):
    tile = lambda x: jnp.concatenate([x] * nb, axis=-1) if nb > 1 else x
    a_lo_t, a_hi_t = tile(alo_ref[...]), tile(ahi_ref[...])
    b_lo_t, b_hi_t = tile(blo_ref[...]), tile(bhi_ref[...])
    cb = u_ref.shape[-1]
    chunk = DFT_COL_CHUNK if cb % DFT_COL_CHUNK == 0 else LANE
    for c in range(cb // chunk):
        sl = slice(c * chunk, (c + 1) * chunk)
        ue = u_ref[0, :, sl]
        uo = u_ref[1, :, sl]
        pe = jnp.dot(ce_ref[...], ue, preferred_element_type=F32)
        po = jnp.dot(co_ref[...], uo, preferred_element_type=F32)
        qe = jnp.dot(se_ref[...], ue, preferred_element_type=F32)
        qo = jnp.dot(so_ref[...], uo, preferred_element_type=F32)
        p_lo, p_hi, q_lo, q_hi = pe + po, pe - po, qe + qo, qo - qe
        a_lo, a_hi, b_lo, b_hi = a_lo_t[:, sl], a_hi_t[:, sl], b_lo_t[:, sl], b_hi_t[:, sl]
        r1_lo = p_lo * a_lo + q_lo * b_lo
        r2_lo = q_lo * a_lo - p_lo * b_lo
        r1_hi = p_hi * a_hi + q_hi * b_hi
        r2_hi = q_hi * a_hi - p_hi * b_hi
        xa_ref[:, sl] = (r1_lo + r1_hi).astype(BF16)
        xb_ref[:, sl] = (r2_lo - r2_hi).astype(BF16)
        xc_ref[:, sl] = (r1_lo - r1_hi).astype(BF16)
        xd_ref[:, sl] = (r2_lo + r2_hi).astype(BF16)

    @pl.when(pl.program_id(1) == 0)
    def _():
        ph = _alt_sign_sum(u_ref[0])
        qh = _alt_sign_sum(u_ref[1])
        a_h = tile(ah_ref[0:1])
        b_h = tile(bh_ref[0:1])
        r1h_ref[...] = jnp.broadcast_to(ph * a_h + qh * b_h, r1h_ref.shape)
        r2h_ref[...] = jnp.broadcast_to(qh * a_h - ph * b_h, r2h_ref.shape)


def _fwd_dft(tabs, u, coefs, order, *, tk, nb):
    ce, se, co, so = tabs[:4]
    alo, ahi, blo, bhi, ah, bh = coefs
    H = ce.shape[0]
    C = HY_WIDTH
    cb = nb * C
    cols = u.shape[-1]
    blk = pl.BlockSpec((tk, H), lambda c, j: (j, 0))
    coef = pl.BlockSpec((tk, C), lambda c, j: (j, order))
    half = pl.BlockSpec((8, C), lambda c, j: (0, order))
    out = pl.BlockSpec((tk, cb), lambda c, j: (j, c))
    out_h = pl.BlockSpec((8, cb), lambda c, j: (0, c))
    shp = jax.ShapeDtypeStruct((H, cols), BF16)
    shp_h = jax.ShapeDtypeStruct((8, cols), F32)
    return pl.pallas_call(
        functools.partial(_fwd_dft_kernel, nb=nb), grid=(cols // cb, H // tk),
        in_specs=[blk, blk, blk, blk,
                  pl.BlockSpec((2, H, cb), lambda c, j: (0, 0, c), pipeline_mode=pl.Buffered(1)),
                  coef, coef, coef, coef, half, half],
        out_specs=[out, out, out, out, out_h, out_h],
        out_shape=[shp, shp, shp, shp, shp_h, shp_h],
        compiler_params=_params(("parallel", "arbitrary")), name="hy_fwd_dft",
    )(ce, se, co, so, u, alo, ahi, blo, bhi, ah, bh)


def _inv_dft_kernel(ce_ref, se_ref, cot_ref, sot_ref, xa_ref, xb_ref, xc_ref, xd_ref,
                    r1h_ref, r2h_ref, g_ref, z_ref):
    tm = ce_ref.shape[0]
    ye = (jnp.dot(ce_ref[...], xa_ref[...], preferred_element_type=F32)
          + jnp.dot(se_ref[...], xb_ref[...], preferred_element_type=F32))
    yo = (jnp.dot(cot_ref[...], xc_ref[...], preferred_element_type=F32)
          + jnp.dot(sot_ref[...], xd_ref[...], preferred_element_type=F32))
    rows = pl.program_id(1) * tm + lax.broadcasted_iota(jnp.int32, ye.shape, 0)
    odd = (rows & 1) == 1
    r1h = r1h_ref[0:1]
    r2h = r2h_ref[0:1]
    ye = ye + jnp.where(odd, -r1h, r1h)
    yo = yo + jnp.where(odd, -r2h, r2h)
    z_ref[0] = (g_ref[0].astype(F32) * ye).astype(z_ref.dtype)
    z_ref[1] = (g_ref[1].astype(F32) * yo).astype(z_ref.dtype)


def _inv_dft(tabs, xs, gate, out_dtype, *, tm, nb):
    ce, se, _, _, cot, sot = tabs
    xa, xb, xc, xd, r1h, r2h = xs
    H = ce.shape[0]
    cb = nb * HY_WIDTH
    cols = xa.shape[-1]
    blk = pl.BlockSpec((tm, H), lambda c, i: (i, 0))
    full = pl.BlockSpec((H, cb), lambda c, i: (0, c), pipeline_mode=pl.Buffered(1))
    half = pl.BlockSpec((8, cb), lambda c, i: (0, c))
    tile = pl.BlockSpec((2, tm, cb), lambda c, i: (0, i, c))
    return pl.pallas_call(
        _inv_dft_kernel, grid=(cols // cb, H // tm),
        in_specs=[blk, blk, blk, blk, full, full, full, full, half, half, tile],
        out_specs=tile, out_shape=jax.ShapeDtypeStruct((2, H, cols), out_dtype),
        compiler_params=_params(("parallel", "arbitrary")), name="hy_inv_dft",
    )(ce, se, cot, sot, xa, xb, xc, xd, r1h, r2h, gate)


def _hyena_features(L):
    deint = lambda a: jnp.concatenate([a[0::2], a[1::2]], axis=0)
    t_idx = deint(jnp.arange(L, dtype=F32))[:, None]
    t_norm = deint(jnp.linspace(0.0, 1.0, L, dtype=F32))[:, None]
    bands = jnp.linspace(1e-4, HY_BANDS - 1, HY_BANDS, dtype=F32)[None, :]
    ang = 2.0 * math.pi * t_idx * bands / L
    z = jnp.concatenate([t_norm, jnp.cos(ang), jnp.sin(ang)], axis=-1)
    z = jnp.pad(z, ((0, 0), (0, LANE - HY_EMB)))
    deltas = jnp.linspace(math.log(HY_TARGET) / HY_SLOW_DECAY,
                          math.log(HY_TARGET) / HY_FAST_DECAY, HY_WIDTH, dtype=F32)
    decay = jnp.exp(-t_norm * jnp.abs(deltas)[None, :])
    return z, decay


def _mix_ffn_kernel(x_ref, ya_ref, yb_ref, yc_ref, gmix_ref, wout_ref, g2_ref, wg_ref, wu_ref,
                    wd_ref, gf_ref, o_ref, yb_scr, *, n_chunks, final_norm):
    gmix = gmix_ref[...]
    ca = ya_ref.shape[-1]
    cb = yb_ref.shape[-1]
    half = yb_ref.shape[1]
    for j in range(cb // LANE):
        sl = slice(j * LANE, (j + 1) * LANE)
        yb_scr[j, pl.ds(0, half, stride=2), :] = yb_ref[0, :, sl].astype(F32)
        yb_scr[j, pl.ds(1, half, stride=2), :] = yb_ref[1, :, sl].astype(F32)
    yb = jnp.concatenate([yb_scr[j] for j in range(cb // LANE)], axis=-1)
    ymix = jnp.concatenate([
        _rms(ya_ref[0].astype(F32), gmix[:, :ca]),
        _rms(yb, gmix[:, ca:ca + cb]),
        _rms(yc_ref[0].astype(F32), gmix[:, ca + cb:]),
    ], axis=-1).astype(BF16)
    x = x_ref[0] + jnp.dot(ymix, wout_ref[...], preferred_element_type=F32)
    h2 = _rms(x, g2_ref[...]).astype(BF16)
    ff = wg_ref.shape[1]
    ch = ff // n_chunks
    acc = x
    for c in range(n_chunks):
        sl = slice(c * ch, (c + 1) * ch)
        gate = jnp.dot(h2, wg_ref[:, sl], preferred_element_type=F32)
        up = jnp.dot(h2, wu_ref[:, sl], preferred_element_type=F32)
        act = (gate * jax.nn.sigmoid(gate) * up).astype(BF16)
        acc = acc + jnp.dot(act, wd_ref[sl, :], preferred_element_type=F32)
    if final_norm:
        acc = _rms(acc, gf_ref[...])
    o_ref[0] = acc


def _mix_ffn(x, ya, yb, yc, gmix, wout, g2, wg, wu, wd, gf, *, tm, final_norm):
    B, S, D = x.shape
    ns = S // tm
    tok = lambda w: pl.BlockSpec((1, tm, w), lambda i: (i // ns, i % ns, 0))
    once = lambda a: pl.BlockSpec(a.shape, lambda i: (0,) * a.ndim,
                                  pipeline_mode=pl.Buffered(1))
    n_chunks = 2 if (wg.shape[1] // 2) % LANE == 0 else 1
    return pl.pallas_call(
        functools.partial(_mix_ffn_kernel, n_chunks=n_chunks, final_norm=final_norm),
        grid=(B * ns,),
        in_specs=[tok(D), tok(ya.shape[-1]),
                  pl.BlockSpec((2, tm // 2, HY_WIDTH), lambda i: (0, i % ns, i // ns)),
                  tok(yc.shape[-1]), once(gmix), once(wout), once(g2), once(wg), once(wu),
                  once(wd), once(gf)],
        out_specs=tok(D), out_shape=jax.ShapeDtypeStruct((B, S, D), F32),
        scratch_shapes=[pltpu.VMEM((HY_WIDTH // LANE, tm, LANE), F32)],
        compiler_params=_params(("parallel",)), name="mix_ffn",
    )(x, ya, yb, yc, gmix, wout, g2, wg, wu, wd, gf)


def _rope_tables(S):
    half = MLA_ROPE // 2
    pos = jnp.arange(S, dtype=F32)
    inv = ROPE_THETA ** (-jnp.arange(0, MLA_ROPE, 2, dtype=F32) / MLA_ROPE)
    ang = pos[:, None] * inv[None, :]
    cos, sin = jnp.cos(ang), jnp.sin(ang)
    z_lo = jnp.zeros((S, MLA_NOPE), F32)
    z_hi = jnp.zeros((S, HEAD_PAD - MLA_NOPE - 2 * half), F32)
    ck = jnp.concatenate([z_lo, cos, cos, z_hi], axis=-1)
    sk = jnp.concatenate([z_lo, -sin, sin, z_hi], axis=-1)
    scale = math.log2(math.e) / math.sqrt(MLA_NOPE + MLA_ROPE)
    cq = jnp.concatenate([jnp.ones((S, MLA_NOPE), F32), cos, cos, z_hi], axis=-1) * scale
    sq = sk * scale
    return cq, sq, ck, sk


def _pad_heads(w, n_heads, width, keep):
    K = w.shape[0]
    w = w.reshape(K, n_heads, width)[:, :, :keep]
    w = jnp.pad(w, ((0, 0), (0, 0), (0, HEAD_PAD - keep)))
    return w.reshape(K, n_heads * HEAD_PAD)


def _layer_weights(w_in, w_uq, w_ukv):
    D = w_in.shape[0]
    o = MLA_Q_RANK + MLA_KV_RANK
    w_kpe = w_in[:, o:o + MLA_ROPE]
    zl = jnp.zeros((D, MLA_NOPE), F32)
    zh = jnp.zeros((D, HEAD_PAD - MLA_NOPE - MLA_ROPE), F32)
    half = MLA_ROPE // 2
    kpe_pad = jnp.concatenate([zl, w_kpe, zh], axis=-1)
    wlat = jnp.concatenate([w_in[:, :o], kpe_pad], axis=-1).astype(BF16)
    o2 = o + MLA_ROPE
    why = w_in[:, o2:o2 + 3 * HY_WIDTH].astype(BF16)
    wna = w_in[:, o2 + 3 * HY_WIDTH:].astype(BF16)
    qd = MLA_NOPE + MLA_ROPE
    wq = _pad_heads(w_uq, MLA_HEADS, qd, qd).astype(BF16)
    kvd = MLA_NOPE + MLA_V
    wk = _pad_heads(w_ukv, MLA_HEADS, kvd, MLA_NOPE).astype(BF16)
    wv = w_ukv.reshape(-1, MLA_HEADS, kvd)[:, :, MLA_NOPE:].reshape(-1, MLA_WIDTH)
    wvt = wv.T.astype(BF16)
    return wlat, why, wna, wq, wk, wvt


def _pad2(a, rows, cols):
    return jnp.pad(a, ((0, rows - a.shape[0]), (0, cols - a.shape[1])))


def kernel(x, norm1_g, w_in, mla_q_norm_g, mla_w_uq, mla_kv_norm_g, mla_w_ukv, hy_conv_w, hy_conv_b, hy_filt_w1, hy_filt_b1, hy_filt_freq1, hy_filt_w2, hy_filt_b2, hy_filt_freq2, hy_filt_w3, hy_skip, na_rpb, mix_norm_g, w_out, norm2_g, ffn_w_gate, ffn_w_up, ffn_w_down, final_norm_g):
    B, S, D = x.shape
    depth = w_in.shape[0]
    L = S
    tm = min(512, S)
    tq = min(512, S)
    tkv = min(256, S)
    kv_unroll = max(1, min(8, S // tkv))
    t_dft = min(256, L // 2)
    nb = next(n for n in (4, 2, 1) if B % n == 0)

    cq, sq, ck, sk = _rope_tables(S)
    tabs = _dft_tables(L)
    z_feat, decay = _hyena_features(L)
    row = lambda v: v.reshape(1, -1)

    for l in range(depth):
        wlat, why, wna, wq, wk, wvt = _layer_weights(w_in[l], mla_w_uq[l], mla_w_ukv[l])
        q, k, vt, hy, naq, nak, nav = _inproj(
            x, row(norm1_g[l]), wlat, why, wna, row(mla_q_norm_g[l]), wq,
            row(mla_kv_norm_g[l]), wk, wvt, cq, sq, ck, sk, tm=tm)

        y_a = _mla_attn(q, k, vt, tq=tq, tk=tkv, unroll=kv_unroll)
        y_c = _natten(naq, nak, nav, _natten_bias_pairs(na_rpb[l]), G=min(16, S // GRID_W))

        u, g1, g2 = _short_conv(hy, hy_conv_w[l], row(hy_conv_b[l]), B)
        hp, hm = _filters(
            z_feat, _pad2(hy_filt_w1[l], LANE, LANE), _pad2(row(hy_filt_b1[l]), 1, LANE),
            _pad2(row(hy_filt_freq1[l]), 1, LANE), _pad2(hy_filt_w2[l], LANE, LANE),
            _pad2(row(hy_filt_b2[l]), 1, LANE), _pad2(row(hy_filt_freq2[l]), 1, LANE),
            _pad2(hy_filt_w3[l], LANE, hy_filt_w3.shape[-1]), decay, tm=tm)
        halves = lambda a: a.reshape(2, L // 2, a.shape[-1])
        coefs = _coefs(tabs, halves(hp), halves(hm), row(hy_skip[l]), tk=t_dft)
        xs = _fwd_dft(tabs, u, coefs, 0, tk=t_dft, nb=nb)
        z1 = _inv_dft(tabs, xs, g1, BF16, tm=t_dft, nb=nb)
        xs = _fwd_dft(tabs, z1, coefs, 1, tk=t_dft, nb=nb)
        y_b = _inv_dft(tabs, xs, g2, BF16, tm=t_dft, nb=nb)

        x = _mix_ffn(x, y_a, y_b, y_c, row(mix_norm_g[l]), w_out[l].astype(BF16),
                     row(norm2_g[l]), ffn_w_gate[l].astype(BF16), ffn_w_up[l].astype(BF16),
                     ffn_w_down[l].astype(BF16), row(final_norm_g),
                     tm=tm, final_norm=(l == depth - 1))
    return x
```

```python
import functools
import math

import jax
import jax.numpy as jnp
import numpy as np
from jax import lax
from jax.experimental import pallas as pl
from jax.experimental.pallas import tpu as pltpu

F32 = jnp.float32
BF16 = jnp.bfloat16

NORM_EPS = 1e-6
MLA_HEADS = 6
MLA_NOPE = 64
MLA_ROPE = 32
MLA_V = 64
MLA_Q_RANK = 256
MLA_KV_RANK = 128
ROPE_THETA = 10000.0
HY_WIDTH = 384
HY_ORDER = 2
HY_BANDS = 8
HY_EMB = 1 + 2 * HY_BANDS
HY_FFN = 64
HY_FAST_DECAY = 0.3
HY_SLOW_DECAY = 1.5
HY_TARGET = 1e-2
NA_HEADS = 4
NA_HEAD_DIM = 64
NA_WIDTH = NA_HEADS * NA_HEAD_DIM
GRID_W = 64
NA_KH = 8
NA_KW = 16
MLA_WIDTH = MLA_HEADS * MLA_V

LANE = 128
HEAD_PAD = 128
F32_SUBLANES = 8
BF16_SUBLANES = 16
MLA_VPAD = MLA_V + BF16_SUBLANES
VMEM_LIMIT = 56 * 1024 * 1024
MASK_VALUE = -1e30
MXU_WIDTH = 256
DFT_COL_CHUNK = MXU_WIDTH


def _params(sem, vmem=VMEM_LIMIT):
    return pltpu.CompilerParams(dimension_semantics=sem, vmem_limit_bytes=vmem)


def _rms(x, g):
    return x * lax.rsqrt(jnp.mean(x * x, axis=-1, keepdims=True) + NORM_EPS) * g


def _const_spec(shape):
    nd = len(shape)
    return pl.BlockSpec(shape, lambda *_: (0,) * nd)


def _dft_table_kernel(ce_ref, se_ref, co_ref, so_ref, cot_ref, sot_ref,
                      eec_ref, ees_ref, eoc_ref, eos_ref, *, L, tk):
    i = pl.program_id(0)
    n = 2 * L
    H = L // 2
    w = 2.0 * math.pi / n

    def angle(prod):
        return (prod & (n - 1)).astype(F32) * w

    @pl.when(i == 0)
    def _():
        r = lax.broadcasted_iota(jnp.int32, (tk, H), 0)
        c = lax.broadcasted_iota(jnp.int32, (tk, H), 1)
        ang_e = angle(r * (2 * c))
        ang_o = angle(r * (2 * c + 1))
        eec_ref[...] = jnp.cos(ang_e)
        ees_ref[...] = jnp.sin(ang_e)
        eoc_ref[...] = jnp.cos(ang_o)
        eos_ref[...] = jnp.sin(ang_o)

    k0 = i * tk
    c1 = lax.broadcasted_iota(jnp.int32, (1, H), 1)

    def rotate(a0, ec_ref, es_ref):
        ca, sa = jnp.cos(a0), jnp.sin(a0)
        ec, es = ec_ref[...], es_ref[...]
        return (ca * ec - sa * es).astype(BF16), (sa * ec + ca * es).astype(BF16)

    ce_ref[...], se_ref[...] = rotate(angle(k0 * (2 * c1)), eec_ref, ees_ref)
    co_ref[...], so_ref[...] = rotate(angle(k0 * (2 * c1 + 1)), eoc_ref, eos_ref)
    cot_ref[...], sot_ref[...] = rotate(angle(c1 * (2 * k0 + 1)), eec_ref, ees_ref)


def _dft_tables(L):
    assert L & (L - 1) == 0, "sequence length must be a power of two"
    H = L // 2
    tk = min(128, H)
    spec = pl.BlockSpec((tk, H), lambda i: (i, 0))
    shp = jax.ShapeDtypeStruct((H, H), BF16)
    return pl.pallas_call(
        functools.partial(_dft_table_kernel, L=L, tk=tk),
        grid=(H // tk,),
        out_specs=[spec] * 6,
        out_shape=[shp] * 6,
        scratch_shapes=[pltpu.VMEM((tk, H), F32)] * 4,
        compiler_params=_params(("arbitrary",)),
        name="dft_tables",
    )()


def _swap_rope_halves(x):
    half = MLA_ROPE // 2
    lane = lax.broadcasted_iota(jnp.int32, (1, x.shape[1]), 1) % HEAD_PAD
    first = (lane >= MLA_NOPE) & (lane < MLA_NOPE + half)
    width = x.shape[1]
    return jnp.where(first, pltpu.roll(x, width - half, axis=1), pltpu.roll(x, half, axis=1))


def _inproj_kernel(x_ref, g1_ref, wlat_ref, why_ref, wna_ref, gq_ref, wq_ref,
                   gkv_ref, wk_ref, wvt_ref, cq_ref, sq_ref, ck_ref, sk_ref,
                   q_ref, k_ref, vt_ref, hy_ref, naq_ref, nak_ref, nav_ref, hy_scr):
    x = x_ref[0]
    h = _rms(x, g1_ref[...]).astype(BF16)
    lat = jnp.dot(h, wlat_ref[...], preferred_element_type=F32)
    hy = jnp.dot(h, why_ref[...], preferred_element_type=F32)
    half = hy.shape[0] // 2
    for j in range(hy.shape[1] // LANE):
        sl = slice(j * LANE, (j + 1) * LANE)
        hy_scr[j] = hy[:, sl]
        hy_ref[0, :, sl] = hy_scr[j, pl.ds(0, half, stride=2), :].astype(BF16)
        hy_ref[1, :, sl] = hy_scr[j, pl.ds(1, half, stride=2), :].astype(BF16)
    na = jnp.dot(h, wna_ref[...], preferred_element_type=F32)
    naq_ref[0] = (na[:, :NA_WIDTH] * (1.0 / math.sqrt(NA_HEAD_DIM))).astype(BF16)
    nak_ref[0] = na[:, NA_WIDTH:2 * NA_WIDTH].astype(BF16)
    nav_ref[0] = na[:, 2 * NA_WIDTH:].astype(BF16)

    c_q = lat[:, :MLA_Q_RANK]
    c_kv = lat[:, MLA_Q_RANK:MLA_Q_RANK + MLA_KV_RANK]
    o = MLA_Q_RANK + MLA_KV_RANK
    kpe = lat[:, o:o + HEAD_PAD]

    cqn = _rms(c_q, gq_ref[...]).astype(BF16)
    qf = jnp.dot(cqn, wq_ref[...], preferred_element_type=F32)
    qs = _swap_rope_halves(qf)
    cq = cq_ref[...]
    sq = sq_ref[...]
    ckvn = _rms(c_kv, gkv_ref[...])
    kf = jnp.dot(ckvn.astype(BF16), wk_ref[...], preferred_element_type=F32)
    kpe_r = kpe * ck_ref[...] + _swap_rope_halves(kpe) * sk_ref[...]
    for hd in range(MLA_HEADS):
        sl = slice(hd * HEAD_PAD, (hd + 1) * HEAD_PAD)
        q_ref[0, :, sl] = (qf[:, sl] * cq + qs[:, sl] * sq).astype(BF16)
        k_ref[0, :, sl] = (kf[:, sl] + kpe_r).astype(BF16)
    vt = jnp.dot(wvt_ref[...], ckvn.T.astype(BF16), preferred_element_type=F32).astype(BF16)
    ones = jnp.ones((MLA_VPAD - MLA_V, vt.shape[1]), BF16)
    for hd in range(MLA_HEADS):
        vt_ref[0, hd * MLA_VPAD:hd * MLA_VPAD + MLA_V, :] = vt[hd * MLA_V:(hd + 1) * MLA_V]
        vt_ref[0, hd * MLA_VPAD + MLA_V:(hd + 1) * MLA_VPAD, :] = ones


def _inproj(x, g1, wlat, why, wna, gq, wq, gkv, wk, wvt, cq, sq, ck, sk, *, tm):
    B, S, D = x.shape
    ns = S // tm
    hyw = why.shape[1]
    qw = MLA_HEADS * HEAD_PAD
    tok = lambda w: pl.BlockSpec((1, tm, w), lambda i: (i // ns, i % ns, 0))
    tab = pl.BlockSpec((tm, HEAD_PAD), lambda i: (i % ns, 0))
    in_specs = [tok(D), _const_spec(g1.shape), _const_spec(wlat.shape), _const_spec(why.shape),
                _const_spec(wna.shape), _const_spec(gq.shape), _const_spec(wq.shape),
                _const_spec(gkv.shape), _const_spec(wk.shape),
                _const_spec(wvt.shape), tab, tab, tab, tab]
    out_specs = [tok(qw), tok(qw),
                 pl.BlockSpec((1, MLA_HEADS * MLA_VPAD, tm), lambda i: (i // ns, 0, i % ns)),
                 pl.BlockSpec((2, tm // 2, hyw), lambda i: (0, i % ns, i // ns)),
                 tok(NA_WIDTH), tok(NA_WIDTH), tok(NA_WIDTH)]
    out_shape = [jax.ShapeDtypeStruct((B, S, qw), BF16),
                 jax.ShapeDtypeStruct((B, S, qw), BF16),
                 jax.ShapeDtypeStruct((B, MLA_HEADS * MLA_VPAD, S), BF16),
                 jax.ShapeDtypeStruct((2, S // 2, B * hyw), BF16),
                 jax.ShapeDtypeStruct((B, S, NA_WIDTH), BF16),
                 jax.ShapeDtypeStruct((B, S, NA_WIDTH), BF16),
                 jax.ShapeDtypeStruct((B, S, NA_WIDTH), BF16)]
    return pl.pallas_call(
        _inproj_kernel, grid=(B * ns,), in_specs=in_specs, out_specs=out_specs,
        out_shape=out_shape, scratch_shapes=[pltpu.VMEM((hyw // LANE, tm, LANE), F32)],
        compiler_params=_params(("parallel",)), name="inproj",
    )(x, g1, wlat, why, wna, gq, wq, gkv, wk, wvt, cq, sq, ck, sk)


def _mla_attn_kernel(q_ref, k_ref, vt_ref, o_ref, m_ref, acc_ref, *, tk, unroll):
    S = k_ref.shape[1]
    nk = S // tk
    m_ref[...] = jnp.full(m_ref.shape, -jnp.inf, F32)
    acc_ref[...] = jnp.zeros(acc_ref.shape, F32)

    def body(j, carry):
        base = j * (unroll * tk)

        def scores(c, hd):
            off = pl.multiple_of(base + c * tk, tk)
            sl = slice(hd * HEAD_PAD, (hd + 1) * HEAD_PAD)
            kj = k_ref[0, pl.ds(off, tk), sl]
            s = lax.dot_general(kj, q_ref[0, :, sl], (((1,), (1,)), ((), ())),
                                preferred_element_type=F32)
            m_old = m_ref[hd]
            m_new = jnp.maximum(m_old, jnp.max(s, axis=0, keepdims=True))
            m_ref[hd] = m_new
            return s, m_old, m_new

        def probs(s, m_old, m_new):
            return jnp.exp2(s - m_new).astype(BF16), jnp.exp2(m_old - m_new)

        def accumulate(c, hd, p, alpha):
            off = pl.multiple_of(base + c * tk, tk)
            vj = vt_ref[0, hd * MLA_VPAD:(hd + 1) * MLA_VPAD, pl.ds(off, tk)]
            acc_ref[hd] = alpha * acc_ref[hd] + jnp.dot(vj, p, preferred_element_type=F32)

        items = [(c, hd) for c in range(unroll) for hd in range(MLA_HEADS)]
        n = len(items)
        st_scores, st_probs = {}, {}
        for t in range(n + 2):
            if t < n:
                st_scores[t] = scores(*items[t])
            if 0 <= t - 1 < n:
                st_probs[t - 1] = probs(*st_scores.pop(t - 1))
            if 0 <= t - 2 < n:
                accumulate(*items[t - 2], *st_probs.pop(t - 2))
        return carry

    lax.fori_loop(0, nk // unroll, body, 0)
    outs = []
    for hd in range(MLA_HEADS):
        acc = acc_ref[hd]
        outs.append(acc[:MLA_V] / acc[MLA_V:MLA_V + 1])
    o_ref[0] = jnp.concatenate(outs, axis=0).T.astype(o_ref.dtype)


def _mla_attn(q, k, vt, *, tq, tk, unroll):
    B, S, qw = q.shape
    vrows = vt.shape[1]
    assert S % (tk * unroll) == 0
    return pl.pallas_call(
        functools.partial(_mla_attn_kernel, tk=tk, unroll=unroll),
        grid=(B, S // tq),
        in_specs=[pl.BlockSpec((1, tq, qw), lambda b, i: (b, i, 0)),
                  pl.BlockSpec((1, S, qw), lambda b, i: (b, 0, 0)),
                  pl.BlockSpec((1, vrows, S), lambda b, i: (b, 0, 0))],
        out_specs=pl.BlockSpec((1, tq, MLA_WIDTH), lambda b, i: (b, i, 0)),
        out_shape=jax.ShapeDtypeStruct((B, S, MLA_WIDTH), BF16),
        scratch_shapes=[pltpu.VMEM((MLA_HEADS, 1, tq), F32),
                        pltpu.VMEM((MLA_HEADS, MLA_VPAD, tq), F32)],
        compiler_params=_params(("parallel", "arbitrary")), name="mla_attn",
    )(q, k, vt)


def _natten_kernel(q_ref, k_ref, v_ref, bias_ref, o_ref, *, R, KH, G):
    col_head = lax.broadcasted_iota(jnp.int32, (1, NA_WIDTH), 1) // NA_HEAD_DIM
    sels = [col_head == hd for hd in range(NA_HEADS)]

    def window(i):
        r = pl.program_id(1) * G + i
        start = jnp.clip(r - KH // 2, 0, R - KH)
        return start - r + (NA_KH - 1), pl.multiple_of(start * GRID_W, GRID_W)

    def scores(i):
        dr0, off = window(i)
        q = q_ref[0, i * GRID_W:(i + 1) * GRID_W, :]
        qm = jnp.concatenate([jnp.where(sel, q, jnp.zeros_like(q)) for sel in sels], axis=0)
        kw = k_ref[0, pl.ds(off, KH * GRID_W), :]
        s = lax.dot_general(qm, kw, (((1,), (1,)), ((), ())),
                            preferred_element_type=F32)
        bias = jnp.concatenate([bias_ref[dr0 + kh] for kh in range(0, KH, 2)], axis=-1)
        logits = s + bias
        return logits, jnp.max(logits, axis=-1, keepdims=True)

    def probs(logits, m):
        p = jnp.exp(logits - m)
        return p.astype(BF16), jnp.sum(p, axis=-1, keepdims=True)

    def output(i, p, l):
        _, off = window(i)
        vw = v_ref[0, pl.ds(off, KH * GRID_W), :]
        o = jnp.dot(p, vw, preferred_element_type=F32) / l
        y = jnp.zeros((GRID_W, NA_WIDTH), F32)
        for hd, sel in enumerate(sels):
            y = y + jnp.where(sel, o[hd * GRID_W:(hd + 1) * GRID_W], 0.0)
        o_ref[0, i * GRID_W:(i + 1) * GRID_W, :] = y.astype(o_ref.dtype)

    st_scores, st_probs = {}, {}
    for t in range(G + 2):
        if t < G:
            st_scores[t] = scores(t)
        if 0 <= t - 1 < G:
            st_probs[t - 1] = probs(*st_scores.pop(t - 1))
        if 0 <= t - 2 < G:
            output(t - 2, *st_probs.pop(t - 2))


def _natten(q, k, v, bias_pairs, *, G):
    B, S, _ = q.shape
    R = S // GRID_W
    KH = min(NA_KH, R)
    assert KH % 2 == 0 and R % G == 0
    return pl.pallas_call(
        functools.partial(_natten_kernel, R=R, KH=KH, G=G),
        grid=(B, R // G),
        in_specs=[pl.BlockSpec((1, G * GRID_W, NA_WIDTH), lambda b, r: (b, r, 0)),
                  pl.BlockSpec((1, S, NA_WIDTH), lambda b, r: (b, 0, 0)),
                  pl.BlockSpec((1, S, NA_WIDTH), lambda b, r: (b, 0, 0)),
                  _const_spec(bias_pairs.shape)],
        out_specs=pl.BlockSpec((1, G * GRID_W, NA_WIDTH), lambda b, r: (b, r, 0)),
        out_shape=jax.ShapeDtypeStruct((B, S, NA_WIDTH), BF16),
        compiler_params=_params(("parallel", "arbitrary")), name="natten",
    )(q, k, v, bias_pairs)


def _natten_bias_pairs(rpb):
    c = np.arange(GRID_W)
    start = np.clip(c - NA_KW // 2, 0, GRID_W - NA_KW)
    v = c[None, :]
    inwin = (v >= start[:, None]) & (v < start[:, None] + NA_KW)
    dc = v - c[:, None] + (NA_KW - 1)
    onehot = (dc[:, :, None] == np.arange(2 * NA_KW - 1)[None, None, :]) & inwin[:, :, None]
    t = jnp.einsum('hdj,wvj->hdwv', rpb.astype(F32), jnp.asarray(onehot, F32),
                   precision=lax.Precision.HIGHEST)
    t = jnp.where(jnp.asarray(inwin)[None, None], t, MASK_VALUE)
    t = jnp.concatenate([t[:, :-1], t[:, 1:]], axis=-1).astype(F32)
    return t.transpose(1, 0, 2, 3).reshape(t.shape[1], NA_HEADS * GRID_W, 2 * GRID_W)


def _alt_sign_sum(x):
    rows, cols = x.shape
    if rows % F32_SUBLANES == 0:
        x = jnp.sum(x.astype(F32).reshape(rows // F32_SUBLANES, F32_SUBLANES, cols), axis=0)
    odd = (lax.broadcasted_iota(jnp.int32, x.shape, 0) & 1) == 1
    xf = x.astype(F32)
    return jnp.sum(jnp.where(odd, -xf, xf), axis=0, keepdims=True)


def _short_conv_kernel(v_ref, x1_ref, x2_ref, wv_ref, w1_ref, w2_ref, bv_ref, b1_ref, b2_ref,
                       u_ref, g1_ref, g2_ref):
    H = v_ref.shape[1]
    i = lax.broadcasted_iota(jnp.int32, v_ref.shape[1:], 0)

    def conv(x_ref, w_ref, b_ref, o_ref):
        xe = x_ref[0].astype(F32)
        xo = x_ref[1].astype(F32)
        xo_prev = jnp.where(i == 0, 0.0, pltpu.roll(xo, 1, axis=0))
        xe_next = jnp.where(i == H - 1, 0.0, pltpu.roll(xe, H - 1, axis=0))
        w = w_ref[...]
        b = b_ref[...]
        o_ref[0] = (w[0:1] * xo_prev + w[1:2] * xe + w[2:3] * xo + b).astype(o_ref.dtype)
        o_ref[1] = (w[0:1] * xe + w[1:2] * xo + w[2:3] * xe_next + b).astype(o_ref.dtype)

    conv(v_ref, wv_ref, bv_ref, u_ref)
    conv(x1_ref, w1_ref, b1_ref, g1_ref)
    conv(x2_ref, w2_ref, b2_ref, g2_ref)


def _short_conv(hy, conv_w, conv_b, B):
    _, H, _ = hy.shape
    C = HY_WIDTH
    nj = C // LANE
    per_b = 3 * nj
    seg = lambda s: pl.BlockSpec((2, H, LANE), lambda b, j: (0, 0, b * per_b + s * nj + j))
    wseg = lambda s: pl.BlockSpec((3, LANE), lambda b, j: (0, s * nj + j))
    bseg = lambda s: pl.BlockSpec((1, LANE), lambda b, j: (0, s * nj + j))
    out = pl.BlockSpec((2, H, LANE), lambda b, j: (0, 0, b * nj + j))
    return pl.pallas_call(
        _short_conv_kernel, grid=(B, nj),
        in_specs=[seg(0), seg(1), seg(2), wseg(0), wseg(1), wseg(2), bseg(0), bseg(1), bseg(2)],
        out_specs=[out, out, out],
        out_shape=[jax.ShapeDtypeStruct((2, H, B * C), BF16)] * 3,
        compiler_params=_params(("parallel", "parallel")), name="hy_short_conv",
    )(hy, hy, hy, conv_w, conv_w, conv_w, conv_b, conv_b, conv_b)


def _filter_kernel(z_ref, w1_ref, b1_ref, f1_ref, w2_ref, b2_ref, f2_ref, w3_ref, dec_ref,
                   hp_ref, hm_ref):
    hi = lax.Precision.HIGHEST
    dot = lambda a, b: jnp.dot(a, b, precision=hi, preferred_element_type=F32)
    h = jnp.sin(f1_ref[...] * (dot(z_ref[...], w1_ref[...]) + b1_ref[...]))
    h = jnp.sin(f2_ref[...] * (dot(h, w2_ref[...]) + b2_ref[...]))
    h = jnp.dot(h.astype(BF16), w3_ref[...].astype(BF16),
                preferred_element_type=F32)
    dec = dec_ref[...]
    C = HY_WIDTH
    for o in range(HY_ORDER):
        hf = h[:, (2 * o) * C:(2 * o + 1) * C] * dec
        hb = h[:, (2 * o + 1) * C:(2 * o + 2) * C] * dec
        hp_ref[:, o * C:(o + 1) * C] = (hf + hb).astype(BF16)
        hm_ref[:, o * C:(o + 1) * C] = (hf - hb).astype(BF16)


def _filters(z, w1, b1, f1, w2, b2, f2, w3, decay, *, tm):
    L = z.shape[0]
    OC = HY_ORDER * HY_WIDTH
    row = lambda w: pl.BlockSpec((tm, w), lambda i: (i, 0))
    return pl.pallas_call(
        _filter_kernel, grid=(L // tm,),
        in_specs=[row(z.shape[1]), _const_spec(w1.shape), _const_spec(b1.shape),
                  _const_spec(f1.shape), _const_spec(w2.shape), _const_spec(b2.shape),
                  _const_spec(f2.shape), _const_spec(w3.shape), row(HY_WIDTH)],
        out_specs=[row(OC), row(OC)],
        out_shape=[jax.ShapeDtypeStruct((L, OC), BF16), jax.ShapeDtypeStruct((L, OC), BF16)],
        compiler_params=_params(("parallel",)), name="hy_filters",
    )(z, w1, b1, f1, w2, b2, f2, w3, decay)


def _coef_kernel(ce_ref, se_ref, co_ref, so_ref, hp_ref, hm_ref, skip_ref,
                 alo_ref, ahi_ref, blo_ref, bhi_ref, ah_ref, bh_ref, *, L):
    i = pl.program_id(0)
    tk = ce_ref.shape[0]
    inv = 1.0 / L
    skip = skip_ref[...]
    pe = jnp.dot(ce_ref[...], hp_ref[0], preferred_element_type=F32)
    po = jnp.dot(co_ref[...], hp_ref[1], preferred_element_type=F32)
    qe = jnp.dot(se_ref[...], hm_ref[0], preferred_element_type=F32)
    qo = jnp.dot(so_ref[...], hm_ref[1], preferred_element_type=F32)
    is0 = (i * tk + lax.broadcasted_iota(jnp.int32, pe.shape, 0)) == 0
    sc = jnp.where(is0, 0.5 * inv, inv)
    alo_ref[...] = sc * (pe + po + skip)
    ahi_ref[...] = sc * (pe - po + skip)
    blo_ref[...] = -inv * (qe + qo)
    bhi_ref[...] = inv * (qe - qo)

    @pl.when(i == 0)
    def _():
        gr = _alt_sign_sum(hp_ref[0]) + skip
        gi = -_alt_sign_sum(hm_ref[1])
        ah_ref[...] = jnp.broadcast_to(inv * gr, ah_ref.shape)
        bh_ref[...] = jnp.broadcast_to(inv * gi, bh_ref.shape)


def _coefs(tabs, hp, hm, skip, *, tk):
    ce, se, co, so = tabs[:4]
    H = ce.shape[0]
    OC = hp.shape[-1]
    blk = pl.BlockSpec((tk, H), lambda i: (i, 0))
    out = pl.BlockSpec((tk, OC), lambda i: (i, 0))
    half = pl.BlockSpec((8, OC), lambda i: (0, 0))
    shp = jax.ShapeDtypeStruct((H, OC), F32)
    shp_h = jax.ShapeDtypeStruct((8, OC), F32)
    return pl.pallas_call(
        functools.partial(_coef_kernel, L=2 * H), grid=(H // tk,),
        in_specs=[blk, blk, blk, blk, _const_spec(hp.shape), _const_spec(hm.shape),
                  _const_spec(skip.shape)],
        out_specs=[out, out, out, out, half, half],
        out_shape=[shp, shp, shp, shp, shp_h, shp_h],
        compiler_params=_params(("arbitrary",)), name="hy_coefs",
    )(ce, se, co, so, hp, hm, skip)


def _fwd_dft_kernel(ce_ref, se_ref, co_ref, so_ref, u_ref, alo_ref, ahi_ref, blo_ref, bhi_ref,
                    ah_ref, bh_ref, xa_ref, xb_ref, xc_ref, xd_ref, r1h_ref, r2h_ref, *, nb):
    tile = lambda x: jnp.concatenate([x] * nb, axis=-1) if nb > 1 else x
    a_lo_t, a_hi_t = tile(alo_ref[...]), tile(ahi_ref[...])
    b_lo_t, b_hi_t = tile(blo_ref[...]), tile(bhi_ref[...])
    cb = u_ref.shape[-1]
    chunk = DFT_COL_CHUNK if cb % DFT_COL_CHUNK == 0 else LANE
    for c in range(cb // chunk):
        sl = slice(c * chunk, (c + 1) * chunk)
        ue = u_ref[0, :, sl]
        uo = u_ref[1, :, sl]
        pe = jnp.dot(ce_ref[...], ue, preferred_element_type=F32)
        po = jnp.dot(co_ref[...], uo, preferred_element_type=F32)
        qe = jnp.dot(se_ref[...], ue, preferred_element_type=F32)
        qo = jnp.dot(so_ref[...], uo, preferred_element_type=F32)
        p_lo, p_hi, q_lo, q_hi = pe + po, pe - po, qe + qo, qo - qe
        a_lo, a_hi, b_lo, b_hi = a_lo_t[:, sl], a_hi_t[:, sl], b_lo_t[:, sl], b_hi_t[:, sl]
        r1_lo = p_lo * a_lo + q_lo * b_lo
        r2_lo = q_lo * a_lo - p_lo * b_lo
        r1_hi = p_hi * a_hi + q_hi * b_hi
        r2_hi = q_hi * a_hi - p_hi * b_hi
        xa_ref[:, sl] = (r1_lo + r1_hi).astype(BF16)
        xb_ref[:, sl] = (r2_lo - r2_hi).astype(BF16)
        xc_ref[:, sl] = (r1_lo - r1_hi).astype(BF16)
        xd_ref[:, sl] = (r2_lo + r2_hi).astype(BF16)

    @pl.when(pl.program_id(1) == 0)
    def _():
        ph = _alt_sign_sum(u_ref[0])
        qh = _alt_sign_sum(u_ref[1])
        a_h = tile(ah_ref[0:1])
        b_h = tile(bh_ref[0:1])
        r1h_ref[...] = jnp.broadcast_to(ph * a_h + qh * b_h, r1h_ref.shape)
        r2h_ref[...] = jnp.broadcast_to(qh * a_h - ph * b_h, r2h_ref.shape)


def _fwd_dft(tabs, u, coefs, order, *, tk, nb):
    ce, se, co, so = tabs[:4]
    alo, ahi, blo, bhi, ah, bh = coefs
    H = ce.shape[0]
    C = HY_WIDTH
    cb = nb * C
    cols = u.shape[-1]
    blk = pl.BlockSpec((tk, H), lambda c, j: (j, 0))
    coef = pl.BlockSpec((tk, C), lambda c, j: (j, order))
    half = pl.BlockSpec((8, C), lambda c, j: (0, order))
    out = pl.BlockSpec((tk, cb), lambda c, j: (j, c))
    out_h = pl.BlockSpec((8, cb), lambda c, j: (0, c))
    shp = jax.ShapeDtypeStruct((H, cols), BF16)
    shp_h = jax.ShapeDtypeStruct((8, cols), F32)
    return pl.pallas_call(
        functools.partial(_fwd_dft_kernel, nb=nb), grid=(cols // cb, H // tk),
        in_specs=[blk, blk, blk, blk,
                  pl.BlockSpec((2, H, cb), lambda c, j: (0, 0, c), pipeline_mode=pl.Buffered(1)),
                  coef, coef, coef, coef, half, half],
        out_specs=[out, out, out, out, out_h, out_h],
        out_shape=[shp, shp, shp, shp, shp_h, shp_h],
        compiler_params=_params(("parallel", "arbitrary")), name="hy_fwd_dft",
    )(ce, se, co, so, u, alo, ahi, blo, bhi, ah, bh)


def _inv_dft_kernel(ce_ref, se_ref, cot_ref, sot_ref, xa_ref, xb_ref, xc_ref, xd_ref,
                    r1h_ref, r2h_ref, g_ref, z_ref):
    tm = ce_ref.shape[0]
    ye = (jnp.dot(ce_ref[...], xa_ref[...], preferred_element_type=F32)
          + jnp.dot(se_ref[...], xb_ref[...], preferred_element_type=F32))
    yo = (jnp.dot(cot_ref[...], xc_ref[...], preferred_element_type=F32)
          + jnp.dot(sot_ref[...], xd_ref[...], preferred_element_type=F32))
    rows = pl.program_id(1) * tm + lax.broadcasted_iota(jnp.int32, ye.shape, 0)
    odd = (rows & 1) == 1
    r1h = r1h_ref[0:1]
    r2h = r2h_ref[0:1]
    ye = ye + jnp.where(odd, -r1h, r1h)
    yo = yo + jnp.where(odd, -r2h, r2h)
    z_ref[0] = (g_ref[0].astype(F32) * ye).astype(z_ref.dtype)
    z_ref[1] = (g_ref[1].astype(F32) * yo).astype(z_ref.dtype)


def _inv_dft(tabs, xs, gate, out_dtype, *, tm, nb):
    ce, se, _, _, cot, sot = tabs
    xa, xb, xc, xd, r1h, r2h = xs
    H = ce.shape[0]
    cb = nb * HY_WIDTH
    cols = xa.shape[-1]
    blk = pl.BlockSpec((tm, H), lambda c, i: (i, 0))
    full = pl.BlockSpec((H, cb), lambda c, i: (0, c), pipeline_mode=pl.Buffered(1))
    half = pl.BlockSpec((8, cb), lambda c, i: (0, c))
    tile = pl.BlockSpec((2, tm, cb), lambda c, i: (0, i, c))
    return pl.pallas_call(
        _inv_dft_kernel, grid=(cols // cb, H // tm),
        in_specs=[blk, blk, blk, blk, full, full, full, full, half, half, tile],
        out_specs=tile, out_shape=jax.ShapeDtypeStruct((2, H, cols), out_dtype),
        compiler_params=_params(("parallel", "arbitrary")), name="hy_inv_dft",
    )(ce, se, cot, sot, xa, xb, xc, xd, r1h, r2h, gate)


def _hyena_features(L):
    deint = lambda a: jnp.concatenate([a[0::2], a[1::2]], axis=0)
    t_idx = deint(jnp.arange(L, dtype=F32))[:, None]
    t_norm = deint(jnp.linspace(0.0, 1.0, L, dtype=F32))[:, None]
    bands = jnp.linspace(1e-4, HY_BANDS - 1, HY_BANDS, dtype=F32)[None, :]
    ang = 2.0 * math.pi * t_idx * bands / L
    z = jnp.concatenate([t_norm, jnp.cos(ang), jnp.sin(ang)], axis=-1)
    z = jnp.pad(z, ((0, 0), (0, LANE - HY_EMB)))
    deltas = jnp.linspace(math.log(HY_TARGET) / HY_SLOW_DECAY,
                          math.log(HY_TARGET) / HY_FAST_DECAY, HY_WIDTH, dtype=F32)
    decay = jnp.exp(-t_norm * jnp.abs(deltas)[None, :])
    return z, decay


def _mix_ffn_kernel(x_ref, ya_ref, yb_ref, yc_ref, gmix_ref, wout_ref, g2_ref, wg_ref, wu_ref,
                    wd_ref, gf_ref, o_ref, yb_scr, *, chunks, final_norm):
    gmix = gmix_ref[...]
    ca = ya_ref.shape[-1]
    cb = yb_ref.shape[-1]
    half = yb_ref.shape[1]
    for j in range(cb // LANE):
        sl = slice(j * LANE, (j + 1) * LANE)
        yb_scr[j, pl.ds(0, half, stride=2), :] = yb_ref[0, :, sl].astype(F32)
        yb_scr[j, pl.ds(1, half, stride=2), :] = yb_ref[1, :, sl].astype(F32)
    yb = jnp.concatenate([yb_scr[j] for j in range(cb // LANE)], axis=-1)
    ymix = jnp.concatenate([
        _rms(ya_ref[0].astype(F32), gmix[:, :ca]),
        _rms(yb, gmix[:, ca:ca + cb]),
        _rms(yc_ref[0].astype(F32), gmix[:, ca + cb:]),
    ], axis=-1).astype(BF16)
    x = x_ref[0] + jnp.dot(ymix, wout_ref[...], preferred_element_type=F32)
    h2 = _rms(x, g2_ref[...]).astype(BF16)
    acc = x
    for lo, hi in chunks:
        sl = slice(lo, hi)
        gate = jnp.dot(h2, wg_ref[:, sl], preferred_element_type=F32)
        up = jnp.dot(h2, wu_ref[:, sl], preferred_element_type=F32)
        act = (gate * jax.nn.sigmoid(gate) * up).astype(BF16)
        acc = acc + jnp.dot(act, wd_ref[sl, :], preferred_element_type=F32)
    if final_norm:
        acc = _rms(acc, gf_ref[...])
    o_ref[0] = acc


def _mix_ffn(x, ya, yb, yc, gmix, wout, g2, wg, wu, wd, gf, *, tm, final_norm):
    B, S, D = x.shape
    ns = S // tm
    tok = lambda w: pl.BlockSpec((1, tm, w), lambda i: (i // ns, i % ns, 0))
    once = lambda a: pl.BlockSpec(a.shape, lambda i: (0,) * a.ndim,
                                  pipeline_mode=pl.Buffered(1))
    ff = wg.shape[1]
    cut = pl.cdiv(pl.cdiv(ff, MXU_WIDTH), 2) * MXU_WIDTH
    chunks = ((0, cut), (cut, ff)) if cut < ff else ((0, ff),)
    return pl.pallas_call(
        functools.partial(_mix_ffn_kernel, chunks=chunks, final_norm=final_norm),
        grid=(B * ns,),
        in_specs=[tok(D), tok(ya.shape[-1]),
                  pl.BlockSpec((2, tm // 2, HY_WIDTH), lambda i: (0, i % ns, i // ns)),
                  tok(yc.shape[-1]), once(gmix), once(wout), once(g2), once(wg), once(wu),
                  once(wd), once(gf)],
        out_specs=tok(D), out_shape=jax.ShapeDtypeStruct((B, S, D), F32),
        scratch_shapes=[pltpu.VMEM((HY_WIDTH // LANE, tm, LANE), F32)],
        compiler_params=_params(("parallel",)), name="mix_ffn",
    )(x, ya, yb, yc, gmix, wout, g2, wg, wu, wd, gf)


def _rope_tables(S):
    half = MLA_ROPE // 2
    pos = jnp.arange(S, dtype=F32)
    inv = ROPE_THETA ** (-jnp.arange(0, MLA_ROPE, 2, dtype=F32) / MLA_ROPE)
    ang = pos[:, None] * inv[None, :]
    cos, sin = jnp.cos(ang), jnp.sin(ang)
    z_lo = jnp.zeros((S, MLA_NOPE), F32)
    z_hi = jnp.zeros((S, HEAD_PAD - MLA_NOPE - 2 * half), F32)
    ck = jnp.concatenate([z_lo, cos, cos, z_hi], axis=-1)
    sk = jnp.concatenate([z_lo, -sin, sin, z_hi], axis=-1)
    scale = math.log2(math.e) / math.sqrt(MLA_NOPE + MLA_ROPE)
    cq = jnp.concatenate([jnp.ones((S, MLA_NOPE), F32), cos, cos, z_hi], axis=-1) * scale
    sq = sk * scale
    return cq, sq, ck, sk


def _pad_heads(w, n_heads, width, keep):
    K = w.shape[0]
    w = w.reshape(K, n_heads, width)[:, :, :keep]
    w = jnp.pad(w, ((0, 0), (0, 0), (0, HEAD_PAD - keep)))
    return w.reshape(K, n_heads * HEAD_PAD)


def _layer_weights(w_in, w_uq, w_ukv):
    D = w_in.shape[0]
    o = MLA_Q_RANK + MLA_KV_RANK
    w_kpe = w_in[:, o:o + MLA_ROPE]
    zl = jnp.zeros((D, MLA_NOPE), F32)
    zh = jnp.zeros((D, HEAD_PAD - MLA_NOPE - MLA_ROPE), F32)
    half = MLA_ROPE // 2
    kpe_pad = jnp.concatenate([zl, w_kpe, zh], axis=-1)
    wlat = jnp.concatenate([w_in[:, :o], kpe_pad], axis=-1).astype(BF16)
    o2 = o + MLA_ROPE
    why = w_in[:, o2:o2 + 3 * HY_WIDTH].astype(BF16)
    wna = w_in[:, o2 + 3 * HY_WIDTH:].astype(BF16)
    qd = MLA_NOPE + MLA_ROPE
    wq = _pad_heads(w_uq, MLA_HEADS, qd, qd).astype(BF16)
    kvd = MLA_NOPE + MLA_V
    wk = _pad_heads(w_ukv, MLA_HEADS, kvd, MLA_NOPE).astype(BF16)
    wv = w_ukv.reshape(-1, MLA_HEADS, kvd)[:, :, MLA_NOPE:].reshape(-1, MLA_WIDTH)
    wvt = wv.T.astype(BF16)
    return wlat, why, wna, wq, wk, wvt


def _pad2(a, rows, cols):
    return jnp.pad(a, ((0, rows - a.shape[0]), (0, cols - a.shape[1])))


def kernel(x, norm1_g, w_in, mla_q_norm_g, mla_w_uq, mla_kv_norm_g, mla_w_ukv, hy_conv_w, hy_conv_b, hy_filt_w1, hy_filt_b1, hy_filt_freq1, hy_filt_w2, hy_filt_b2, hy_filt_freq2, hy_filt_w3, hy_skip, na_rpb, mix_norm_g, w_out, norm2_g, ffn_w_gate, ffn_w_up, ffn_w_down, final_norm_g):
    B, S, D = x.shape
    depth = w_in.shape[0]
    L = S
    tm = min(512, S)
    tq = min(512, S)
    tkv = min(256, S)
    kv_unroll = max(1, min(8, S // tkv))
    t_dft = min(256, L // 2)
    nb = next(n for n in (4, 2, 1) if B % n == 0)

    cq, sq, ck, sk = _rope_tables(S)
    tabs = _dft_tables(L)
    z_feat, decay = _hyena_features(L)
    row = lambda v: v.reshape(1, -1)

    for l in range(depth):
        wlat, why, wna, wq, wk, wvt = _layer_weights(w_in[l], mla_w_uq[l], mla_w_ukv[l])
        q, k, vt, hy, naq, nak, nav = _inproj(
            x, row(norm1_g[l]), wlat, why, wna, row(mla_q_norm_g[l]), wq,
            row(mla_kv_norm_g[l]), wk, wvt, cq, sq, ck, sk, tm=tm)

        y_a = _mla_attn(q, k, vt, tq=tq, tk=tkv, unroll=kv_unroll)
        y_c = _natten(naq, nak, nav, _natten_bias_pairs(na_rpb[l]), G=min(16, S // GRID_W))

        u, g1, g2 = _short_conv(hy, hy_conv_w[l], row(hy_conv_b[l]), B)
        hp, hm = _filters(
            z_feat, _pad2(hy_filt_w1[l], LANE, LANE), _pad2(row(hy_filt_b1[l]), 1, LANE),
            _pad2(row(hy_filt_freq1[l]), 1, LANE), _pad2(hy_filt_w2[l], LANE, LANE),
            _pad2(row(hy_filt_b2[l]), 1, LANE), _pad2(row(hy_filt_freq2[l]), 1, LANE),
            _pad2(hy_filt_w3[l], LANE, hy_filt_w3.shape[-1]), decay, tm=tm)
        halves = lambda a: a.reshape(2, L // 2, a.shape[-1])
        coefs = _coefs(tabs, halves(hp), halves(hm), row(hy_skip[l]), tk=t_dft)
        xs = _fwd_dft(tabs, u, coefs, 0, tk=t_dft, nb=nb)
        z1 = _inv_dft(tabs, xs, g1, BF16, tm=t_dft, nb=nb)
        xs = _fwd_dft(tabs, z1, coefs, 1, tk=t_dft, nb=nb)
        y_b = _inv_dft(tabs, xs, g2, BF16, tm=t_dft, nb=nb)

        x = _mix_ffn(x, y_a, y_b, y_c, row(mix_norm_g[l]), w_out[l].astype(BF16),
                     row(norm2_g[l]), ffn_w_gate[l].astype(BF16), ffn_w_up[l].astype(BF16),
                     ffn_w_down[l].astype(BF16), row(final_norm_g),
                     tm=tm, final_norm=(l == depth - 1))
    return x
```

```python
import functools
import math

import jax
import jax.numpy as jnp
import numpy as np
from jax import lax
from jax.experimental import pallas as pl
from jax.experimental.pallas import tpu as pltpu

F32 = jnp.float32
BF16 = jnp.bfloat16

NORM_EPS = 1e-6
MLA_HEADS = 6
MLA_NOPE = 64
MLA_ROPE = 32
MLA_V = 64
MLA_Q_RANK = 256
MLA_KV_RANK = 128
ROPE_THETA = 10000.0
HY_WIDTH = 384
HY_ORDER = 2
HY_BANDS = 8
HY_EMB = 1 + 2 * HY_BANDS
HY_FFN = 64
HY_FAST_DECAY = 0.3
HY_SLOW_DECAY = 1.5
HY_TARGET = 1e-2
NA_HEADS = 4
NA_HEAD_DIM = 64
NA_WIDTH = NA_HEADS * NA_HEAD_DIM
GRID_W = 64
NA_KH = 8
NA_KW = 16
MLA_WIDTH = MLA_HEADS * MLA_V

LANE = 128
HEAD_PAD = 128
F32_SUBLANES = 8
BF16_SUBLANES = 16
MLA_VPAD = MLA_V + BF16_SUBLANES
VMEM_LIMIT = 56 * 1024 * 1024
MASK_VALUE = -1e30
MXU_WIDTH = 256
DFT_COL_CHUNK = MXU_WIDTH


def _params(sem, vmem=VMEM_LIMIT):
    return pltpu.CompilerParams(dimension_semantics=sem, vmem_limit_bytes=vmem)


def _rms(x, g):
    return x * lax.rsqrt(jnp.mean(x * x, axis=-1, keepdims=True) + NORM_EPS) * g


def _const_spec(shape):
    nd = len(shape)
    return pl.BlockSpec(shape, lambda *_: (0,) * nd)


def _dft_table_kernel(ce_ref, se_ref, co_ref, so_ref, cot_ref, sot_ref,
                      eec_ref, ees_ref, eoc_ref, eos_ref, *, L, tk):
    i = pl.program_id(0)
    n = 2 * L
    H = L // 2
    w = 2.0 * math.pi / n

    def angle(prod):
        return (prod & (n - 1)).astype(F32) * w

    @pl.when(i == 0)
    def _():
        r = lax.broadcasted_iota(jnp.int32, (tk, H), 0)
        c = lax.broadcasted_iota(jnp.int32, (tk, H), 1)
        ang_e = angle(r * (2 * c))
        ang_o = angle(r * (2 * c + 1))
        eec_ref[...] = jnp.cos(ang_e)
        ees_ref[...] = jnp.sin(ang_e)
        eoc_ref[...] = jnp.cos(ang_o)
        eos_ref[...] = jnp.sin(ang_o)

    k0 = i * tk
    c1 = lax.broadcasted_iota(jnp.int32, (1, H), 1)

    def rotate(a0, ec_ref, es_ref):
        ca, sa = jnp.cos(a0), jnp.sin(a0)
        ec, es = ec_ref[...], es_ref[...]
        return (ca * ec - sa * es).astype(BF16), (sa * ec + ca * es).astype(BF16)

    ce_ref[...], se_ref[...] = rotate(angle(k0 * (2 * c1)), eec_ref, ees_ref)
    co_ref[...], so_ref[...] = rotate(angle(k0 * (2 * c1 + 1)), eoc_ref, eos_ref)
    cot_ref[...], sot_ref[...] = rotate(angle(c1 * (2 * k0 + 1)), eec_ref, ees_ref)


def _dft_tables(L):
    assert L & (L - 1) == 0, "sequence length must be a power of two"
    H = L // 2
    tk = min(128, H)
    spec = pl.BlockSpec((tk, H), lambda i: (i, 0))
    shp = jax.ShapeDtypeStruct((H, H), BF16)
    return pl.pallas_call(
        functools.partial(_dft_table_kernel, L=L, tk=tk),
        grid=(H // tk,),
        out_specs=[spec] * 6,
        out_shape=[shp] * 6,
        scratch_shapes=[pltpu.VMEM((tk, H), F32)] * 4,
        compiler_params=_params(("arbitrary",)),
        name="dft_tables",
    )()


def _swap_rope_halves(x):
    half = MLA_ROPE // 2
    lane = lax.broadcasted_iota(jnp.int32, (1, x.shape[1]), 1) % HEAD_PAD
    first = (lane >= MLA_NOPE) & (lane < MLA_NOPE + half)
    width = x.shape[1]
    return jnp.where(first, pltpu.roll(x, width - half, axis=1), pltpu.roll(x, half, axis=1))


def _inproj_kernel(x_ref, g1_ref, wlat_ref, why_ref, wna_ref, gq_ref, wq_ref,
                   gkv_ref, wk_ref, wvt_ref, cq_ref, sq_ref, ck_ref, sk_ref,
                   q_ref, k_ref, vt_ref, hy_ref, naq_ref, nak_ref, nav_ref, hy_scr):
    x = x_ref[0]
    h = _rms(x, g1_ref[...]).astype(BF16)
    lat = jnp.dot(h, wlat_ref[...], preferred_element_type=F32)
    hy = jnp.dot(h, why_ref[...], preferred_element_type=F32)
    half = hy.shape[0] // 2
    for j in range(hy.shape[1] // LANE):
        sl = slice(j * LANE, (j + 1) * LANE)
        hy_scr[j] = hy[:, sl]
        hy_ref[0, :, sl] = hy_scr[j, pl.ds(0, half, stride=2), :].astype(BF16)
        hy_ref[1, :, sl] = hy_scr[j, pl.ds(1, half, stride=2), :].astype(BF16)
    na = jnp.dot(h, wna_ref[...], preferred_element_type=F32)
    naq_ref[0] = (na[:, :NA_WIDTH] * (1.0 / math.sqrt(NA_HEAD_DIM))).astype(BF16)
    nak_ref[0] = na[:, NA_WIDTH:2 * NA_WIDTH].astype(BF16)
    nav_ref[0] = na[:, 2 * NA_WIDTH:].astype(BF16)

    c_q = lat[:, :MLA_Q_RANK]
    c_kv = lat[:, MLA_Q_RANK:MLA_Q_RANK + MLA_KV_RANK]
    o = MLA_Q_RANK + MLA_KV_RANK
    kpe = lat[:, o:o + HEAD_PAD]

    cqn = _rms(c_q, gq_ref[...]).astype(BF16)
    qf = jnp.dot(cqn, wq_ref[...], preferred_element_type=F32)
    qs = _swap_rope_halves(qf)
    cq = cq_ref[...]
    sq = sq_ref[...]
    ckvn = _rms(c_kv, gkv_ref[...])
    kf = jnp.dot(ckvn.astype(BF16), wk_ref[...], preferred_element_type=F32)
    kpe_r = kpe * ck_ref[...] + _swap_rope_halves(kpe) * sk_ref[...]
    for hd in range(MLA_HEADS):
        sl = slice(hd * HEAD_PAD, (hd + 1) * HEAD_PAD)
        q_ref[0, :, sl] = (qf[:, sl] * cq + qs[:, sl] * sq).astype(BF16)
        k_ref[0, :, sl] = (kf[:, sl] + kpe_r).astype(BF16)
    vt = jnp.dot(wvt_ref[...], ckvn.T.astype(BF16), preferred_element_type=F32).astype(BF16)
    ones = jnp.ones((MLA_VPAD - MLA_V, vt.shape[1]), BF16)
    for hd in range(MLA_HEADS):
        vt_ref[0, hd * MLA_VPAD:hd * MLA_VPAD + MLA_V, :] = vt[hd * MLA_V:(hd + 1) * MLA_V]
        vt_ref[0, hd * MLA_VPAD + MLA_V:(hd + 1) * MLA_VPAD, :] = ones


def _inproj(x, g1, wlat, why, wna, gq, wq, gkv, wk, wvt, cq, sq, ck, sk, *, tm):
    B, S, D = x.shape
    ns = S // tm
    hyw = why.shape[1]
    qw = MLA_HEADS * HEAD_PAD
    tok = lambda w: pl.BlockSpec((1, tm, w), lambda i: (i // ns, i % ns, 0))
    tab = pl.BlockSpec((tm, HEAD_PAD), lambda i: (i % ns, 0))
    in_specs = [tok(D), _const_spec(g1.shape), _const_spec(wlat.shape), _const_spec(why.shape),
                _const_spec(wna.shape), _const_spec(gq.shape), _const_spec(wq.shape),
                _const_spec(gkv.shape), _const_spec(wk.shape),
                _const_spec(wvt.shape), tab, tab, tab, tab]
    out_specs = [tok(qw), tok(qw),
                 pl.BlockSpec((1, MLA_HEADS * MLA_VPAD, tm), lambda i: (i // ns, 0, i % ns)),
                 pl.BlockSpec((2, tm // 2, hyw), lambda i: (0, i % ns, i // ns)),
                 tok(NA_WIDTH), tok(NA_WIDTH), tok(NA_WIDTH)]
    out_shape = [jax.ShapeDtypeStruct((B, S, qw), BF16),
                 jax.ShapeDtypeStruct((B, S, qw), BF16),
                 jax.ShapeDtypeStruct((B, MLA_HEADS * MLA_VPAD, S), BF16),
                 jax.ShapeDtypeStruct((2, S // 2, B * hyw), BF16),
                 jax.ShapeDtypeStruct((B, S, NA_WIDTH), BF16),
                 jax.ShapeDtypeStruct((B, S, NA_WIDTH), BF16),
                 jax.ShapeDtypeStruct((B, S, NA_WIDTH), BF16)]
    return pl.pallas_call(
        _inproj_kernel, grid=(B * ns,), in_specs=in_specs, out_specs=out_specs,
        out_shape=out_shape, scratch_shapes=[pltpu.VMEM((hyw // LANE, tm, LANE), F32)],
        compiler_params=_params(("parallel",)), name="inproj",
    )(x, g1, wlat, why, wna, gq, wq, gkv, wk, wvt, cq, sq, ck, sk)


def _mla_attn_kernel(q_ref, k_ref, vt_ref, o_ref, m_ref, acc_ref, *, tk, unroll):
    S = k_ref.shape[1]
    nk = S // tk
    m_ref[...] = jnp.full(m_ref.shape, -jnp.inf, F32)
    acc_ref[...] = jnp.zeros(acc_ref.shape, F32)

    def body(j, carry):
        base = j * (unroll * tk)

        def scores(c, hd):
            off = pl.multiple_of(base + c * tk, tk)
            sl = slice(hd * HEAD_PAD, (hd + 1) * HEAD_PAD)
            kj = k_ref[0, pl.ds(off, tk), sl]
            s = lax.dot_general(kj, q_ref[0, :, sl], (((1,), (1,)), ((), ())),
                                preferred_element_type=F32)
            m_old = m_ref[hd]
            m_new = jnp.maximum(m_old, jnp.max(s, axis=0, keepdims=True))
            m_ref[hd] = m_new
            return s, m_old, m_new

        def probs(s, m_old, m_new):
            return jnp.exp2(s - m_new).astype(BF16), jnp.exp2(m_old - m_new)

        def accumulate(c, hd, p, alpha):
            off = pl.multiple_of(base + c * tk, tk)
            vj = vt_ref[0, hd * MLA_VPAD:(hd + 1) * MLA_VPAD, pl.ds(off, tk)]
            acc_ref[hd] = alpha * acc_ref[hd] + jnp.dot(vj, p, preferred_element_type=F32)

        items = [(c, hd) for c in range(unroll) for hd in range(MLA_HEADS)]
        n = len(items)
        st_scores, st_probs = {}, {}
        for t in range(n + 2):
            if t < n:
                st_scores[t] = scores(*items[t])
            if 0 <= t - 1 < n:
                st_probs[t - 1] = probs(*st_scores.pop(t - 1))
            if 0 <= t - 2 < n:
                accumulate(*items[t - 2], *st_probs.pop(t - 2))
        return carry

    lax.fori_loop(0, nk // unroll, body, 0)
    outs = []
    for hd in range(MLA_HEADS):
        acc = acc_ref[hd]
        outs.append(acc[:MLA_V] / acc[MLA_V:MLA_V + 1])
    o_ref[0] = jnp.concatenate(outs, axis=0).T.astype(o_ref.dtype)


def _mla_attn(q, k, vt, *, tq, tk, unroll):
    B, S, qw = q.shape
    vrows = vt.shape[1]
    assert S % (tk * unroll) == 0
    return pl.pallas_call(
        functools.partial(_mla_attn_kernel, tk=tk, unroll=unroll),
        grid=(B, S // tq),
        in_specs=[pl.BlockSpec((1, tq, qw), lambda b, i: (b, i, 0)),
                  pl.BlockSpec((1, S, qw), lambda b, i: (b, 0, 0)),
                  pl.BlockSpec((1, vrows, S), lambda b, i: (b, 0, 0))],
        out_specs=pl.BlockSpec((1, tq, MLA_WIDTH), lambda b, i: (b, i, 0)),
        out_shape=jax.ShapeDtypeStruct((B, S, MLA_WIDTH), BF16),
        scratch_shapes=[pltpu.VMEM((MLA_HEADS, 1, tq), F32),
                        pltpu.VMEM((MLA_HEADS, MLA_VPAD, tq), F32)],
        compiler_params=_params(("parallel", "arbitrary")), name="mla_attn",
    )(q, k, vt)


def _natten_kernel(q_ref, k_ref, v_ref, bias_ref, o_ref, *, R, KH, G):
    col_head = lax.broadcasted_iota(jnp.int32, (1, NA_WIDTH), 1) // NA_HEAD_DIM
    sels = [col_head == hd for hd in range(NA_HEADS)]

    def window(i):
        r = pl.program_id(1) * G + i
        start = jnp.clip(r - KH // 2, 0, R - KH)
        return start - r + (NA_KH - 1), pl.multiple_of(start * GRID_W, GRID_W)

    def scores(i):
        dr0, off = window(i)
        q = q_ref[0, i * GRID_W:(i + 1) * GRID_W, :]
        qm = jnp.concatenate([jnp.where(sel, q, jnp.zeros_like(q)) for sel in sels], axis=0)
        kw = k_ref[0, pl.ds(off, KH * GRID_W), :]
        s = lax.dot_general(qm, kw, (((1,), (1,)), ((), ())),
                            preferred_element_type=F32)
        bias = jnp.concatenate([bias_ref[dr0 + kh] for kh in range(0, KH, 2)], axis=-1)
        logits = s + bias
        return logits, jnp.max(logits, axis=-1, keepdims=True)

    def probs(logits, m):
        p = jnp.exp(logits - m)
        return p.astype(BF16), jnp.sum(p, axis=-1, keepdims=True)

    def output(i, p, l):
        _, off = window(i)
        vw = v_ref[0, pl.ds(off, KH * GRID_W), :]
        o = jnp.dot(p, vw, preferred_element_type=F32) / l
        y = jnp.zeros((GRID_W, NA_WIDTH), F32)
        for hd, sel in enumerate(sels):
            y = y + jnp.where(sel, o[hd * GRID_W:(hd + 1) * GRID_W], 0.0)
        o_ref[0, i * GRID_W:(i + 1) * GRID_W, :] = y.astype(o_ref.dtype)

    st_scores, st_probs = {}, {}
    for t in range(G + 2):
        if t < G:
            st_scores[t] = scores(t)
        if 0 <= t - 1 < G:
            st_probs[t - 1] = probs(*st_scores.pop(t - 1))
        if 0 <= t - 2 < G:
            output(t - 2, *st_probs.pop(t - 2))


def _natten(q, k, v, bias_pairs, *, G):
    B, S, _ = q.shape
    R = S // GRID_W
    KH = min(NA_KH, R)
    assert KH % 2 == 0 and R % G == 0
    return pl.pallas_call(
        functools.partial(_natten_kernel, R=R, KH=KH, G=G),
        grid=(B, R // G),
        in_specs=[pl.BlockSpec((1, G * GRID_W, NA_WIDTH), lambda b, r: (b, r, 0)),
                  pl.BlockSpec((1, S, NA_WIDTH), lambda b, r: (b, 0, 0)),
                  pl.BlockSpec((1, S, NA_WIDTH), lambda b, r: (b, 0, 0)),
                  _const_spec(bias_pairs.shape)],
        out_specs=pl.BlockSpec((1, G * GRID_W, NA_WIDTH), lambda b, r: (b, r, 0)),
        out_shape=jax.ShapeDtypeStruct((B, S, NA_WIDTH), BF16),
        compiler_params=_params(("parallel", "arbitrary")), name="natten",
    )(q, k, v, bias_pairs)


def _natten_bias_pairs(rpb):
    c = np.arange(GRID_W)
    start = np.clip(c - NA_KW // 2, 0, GRID_W - NA_KW)
    v = c[None, :]
    inwin = (v >= start[:, None]) & (v < start[:, None] + NA_KW)
    dc = v - c[:, None] + (NA_KW - 1)
    onehot = (dc[:, :, None] == np.arange(2 * NA_KW - 1)[None, None, :]) & inwin[:, :, None]
    t = jnp.einsum('hdj,wvj->hdwv', rpb.astype(F32), jnp.asarray(onehot, F32),
                   precision=lax.Precision.HIGHEST)
    t = jnp.where(jnp.asarray(inwin)[None, None], t, MASK_VALUE)
    t = jnp.concatenate([t[:, :-1], t[:, 1:]], axis=-1).astype(F32)
    return t.transpose(1, 0, 2, 3).reshape(t.shape[1], NA_HEADS * GRID_W, 2 * GRID_W)


def _alt_sign_sum(x):
    rows, cols = x.shape
    if rows % F32_SUBLANES == 0:
        x = jnp.sum(x.astype(F32).reshape(rows // F32_SUBLANES, F32_SUBLANES, cols), axis=0)
    odd = (lax.broadcasted_iota(jnp.int32, x.shape, 0) & 1) == 1
    xf = x.astype(F32)
    return jnp.sum(jnp.where(odd, -xf, xf), axis=0, keepdims=True)


def _short_conv_kernel(v_ref, x1_ref, x2_ref, wv_ref, w1_ref, w2_ref, bv_ref, b1_ref, b2_ref,
                       u_ref, g1_ref, g2_ref):
    H = v_ref.shape[1]
    i = lax.broadcasted_iota(jnp.int32, v_ref.shape[1:], 0)

    def conv(x_ref, w_ref, b_ref, o_ref):
        xe = x_ref[0].astype(F32)
        xo = x_ref[1].astype(F32)
        xo_prev = jnp.where(i == 0, 0.0, pltpu.roll(xo, 1, axis=0))
        xe_next = jnp.where(i == H - 1, 0.0, pltpu.roll(xe, H - 1, axis=0))
        w = w_ref[...]
        b = b_ref[...]
        o_ref[0] = (w[0:1] * xo_prev + w[1:2] * xe + w[2:3] * xo + b).astype(o_ref.dtype)
        o_ref[1] = (w[0:1] * xe + w[1:2] * xo + w[2:3] * xe_next + b).astype(o_ref.dtype)

    conv(v_ref, wv_ref, bv_ref, u_ref)
    conv(x1_ref, w1_ref, b1_ref, g1_ref)
    conv(x2_ref, w2_ref, b2_ref, g2_ref)


def _short_conv(hy, conv_w, conv_b, B):
    _, H, _ = hy.shape
    C = HY_WIDTH
    nj = C // LANE
    per_b = 3 * nj
    seg = lambda s: pl.BlockSpec((2, H, LANE), lambda b, j: (0, 0, b * per_b + s * nj + j))
    wseg = lambda s: pl.BlockSpec((3, LANE), lambda b, j: (0, s * nj + j))
    bseg = lambda s: pl.BlockSpec((1, LANE), lambda b, j: (0, s * nj + j))
    out = pl.BlockSpec((2, H, LANE), lambda b, j: (0, 0, b * nj + j))
    return pl.pallas_call(
        _short_conv_kernel, grid=(B, nj),
        in_specs=[seg(0), seg(1), seg(2), wseg(0), wseg(1), wseg(2), bseg(0), bseg(1), bseg(2)],
        out_specs=[out, out, out],
        out_shape=[jax.ShapeDtypeStruct((2, H, B * C), BF16)] * 3,
        compiler_params=_params(("parallel", "parallel")), name="hy_short_conv",
    )(hy, hy, hy, conv_w, conv_w, conv_w, conv_b, conv_b, conv_b)


def _filter_kernel(z_ref, w1_ref, b1_ref, f1_ref, w2_ref, b2_ref, f2_ref, w3_ref, dec_ref,
                   hp_ref, hm_ref):
    hi = lax.Precision.HIGHEST
    dot = lambda a, b: jnp.dot(a, b, precision=hi, preferred_element_type=F32)
    h = jnp.sin(f1_ref[...] * (dot(z_ref[...], w1_ref[...]) + b1_ref[...]))
    h = jnp.sin(f2_ref[...] * (dot(h, w2_ref[...]) + b2_ref[...]))
    h = jnp.dot(h.astype(BF16), w3_ref[...].astype(BF16),
                preferred_element_type=F32)
    dec = dec_ref[...]
    C = HY_WIDTH
    for o in range(HY_ORDER):
        hf = h[:, (2 * o) * C:(2 * o + 1) * C] * dec
        hb = h[:, (2 * o + 1) * C:(2 * o + 2) * C] * dec
        hp_ref[:, o * C:(o + 1) * C] = (hf + hb).astype(BF16)
        hm_ref[:, o * C:(o + 1) * C] = (hf - hb).astype(BF16)


def _filters(z, w1, b1, f1, w2, b2, f2, w3, decay, *, tm):
    L = z.shape[0]
    OC = HY_ORDER * HY_WIDTH
    row = lambda w: pl.BlockSpec((tm, w), lambda i: (i, 0))
    return pl.pallas_call(
        _filter_kernel, grid=(L // tm,),
        in_specs=[row(z.shape[1]), _const_spec(w1.shape), _const_spec(b1.shape),
                  _const_spec(f1.shape), _const_spec(w2.shape), _const_spec(b2.shape),
                  _const_spec(f2.shape), _const_spec(w3.shape), row(HY_WIDTH)],
        out_specs=[row(OC), row(OC)],
        out_shape=[jax.ShapeDtypeStruct((L, OC), BF16), jax.ShapeDtypeStruct((L, OC), BF16)],
        compiler_params=_params(("parallel",)), name="hy_filters",
    )(z, w1, b1, f1, w2, b2, f2, w3, decay)


def _coef_kernel(ce_ref, se_ref, co_ref, so_ref, hp_ref, hm_ref, skip_ref,
                 alo_ref, ahi_ref, blo_ref, bhi_ref, ah_ref, bh_ref, *, L):
    i = pl.program_id(0)
    tk = ce_ref.shape[0]
    inv = 1.0 / L
    skip = skip_ref[...]
    pe = jnp.dot(ce_ref[...], hp_ref[0], preferred_element_type=F32)
    po = jnp.dot(co_ref[...], hp_ref[1], preferred_element_type=F32)
    qe = jnp.dot(se_ref[...], hm_ref[0], preferred_element_type=F32)
    qo = jnp.dot(so_ref[...], hm_ref[1], preferred_element_type=F32)
    is0 = (i * tk + lax.broadcasted_iota(jnp.int32, pe.shape, 0)) == 0
    sc = jnp.where(is0, 0.5 * inv, inv)
    alo_ref[...] = sc * (pe + po + skip)
    ahi_ref[...] = sc * (pe - po + skip)
    blo_ref[...] = -inv * (qe + qo)
    bhi_ref[...] = inv * (qe - qo)

    @pl.when(i == 0)
    def _():
        gr = _alt_sign_sum(hp_ref[0]) + skip
        gi = -_alt_sign_sum(hm_ref[1])
        ah_ref[...] = jnp.broadcast_to(inv * gr, ah_ref.shape)
        bh_ref[...] = jnp.broadcast_to(inv * gi, bh_ref.shape)


def _coefs(tabs, hp, hm, skip, *, tk):
    ce, se, co, so = tabs[:4]
    H = ce.shape[0]
    OC = hp.shape[-1]
    blk = pl.BlockSpec((tk, H), lambda i: (i, 0))
    out = pl.BlockSpec((tk, OC), lambda i: (i, 0))
    half = pl.BlockSpec((8, OC), lambda i: (0, 0))
    shp = jax.ShapeDtypeStruct((H, OC), F32)
    shp_h = jax.ShapeDtypeStruct((8, OC), F32)
    return pl.pallas_call(
        functools.partial(_coef_kernel, L=2 * H), grid=(H // tk,),
        in_specs=[blk, blk, blk, blk, _const_spec(hp.shape), _const_spec(hm.shape),
                  _const_spec(skip.shape)],
        out_specs=[out, out, out, out, half, half],
        out_shape=[shp, shp, shp, shp, shp_h, shp_h],
        compiler_params=_params(("arbitrary",)), name="hy_coefs",
    )(ce, se, co, so, hp, hm, skip)


def _fwd_dft_kernel(ce_ref, se_ref, co_ref, so_ref, u_ref, alo_ref, ahi_ref, blo_ref, bhi_ref,
                    ah_ref, bh_ref, xa_ref, xb_ref, xc_ref, xd_ref, r1h_ref, r2h_ref, *, nb):
    tile = lambda x: jnp.concatenate([x] * nb, axis=-1) if nb > 1 else x
    a_lo_t, a_hi_t = tile(alo_ref[...]), tile(ahi_ref[...])
    b_lo_t, b_hi_t = tile(blo_ref[...]), tile(bhi_ref[...])
    cb = u_ref.shape[-1]
    chunk = DFT_COL_CHUNK if cb % DFT_COL_CHUNK == 0 else LANE
    for c in range(cb // chunk):
        sl = slice(c * chunk, (c + 1) * chunk)
        ue = u_ref[0, :, sl]
        uo = u_ref[1, :, sl]
        pe = jnp.dot(ce_ref[...], ue, preferred_element_type=F32)
        po = jnp.dot(co_ref[...], uo, preferred_element_type=F32)
        qe = jnp.dot(se_ref[...], ue, preferred_element_type=F32)
        qo = jnp.dot(so_ref[...], uo, preferred_element_type=F32)
        p_lo, p_hi, q_lo, q_hi = pe + po, pe - po, qe + qo, qo - qe
        a_lo, a_hi, b_lo, b_hi = a_lo_t[:, sl], a_hi_t[:, sl], b_lo_t[:, sl], b_hi_t[:, sl]
        r1_lo = p_lo * a_lo + q_lo * b_lo
        r2_lo = q_lo * a_lo - p_lo * b_lo
        r1_hi = p_hi * a_hi + q_hi * b_hi
        r2_hi = q_hi * a_hi - p_hi * b_hi
        xa_ref[:, sl] = (r1_lo + r1_hi).astype(BF16)
        xb_ref[:, sl] = (r2_lo - r2_hi).astype(BF16)
        xc_ref[:, sl] = (r1_lo - r1_hi).astype(BF16)
        xd_ref[:, sl] = (r2_lo + r2_hi).astype(BF16)

    @pl.when(pl.program_id(1) == 0)
    def _():
        ph = _alt_sign_sum(u_ref[0])
        qh = _alt_sign_sum(u_ref[1])
        a_h = tile(ah_ref[0:1])
        b_h = tile(bh_ref[0:1])
        r1h_ref[...] = jnp.broadcast_to(ph * a_h + qh * b_h, r1h_ref.shape)
        r2h_ref[...] = jnp.broadcast_to(qh * a_h - ph * b_h, r2h_ref.shape)


def _fwd_dft(tabs, u, coefs, order, *, tk, nb):
    ce, se, co, so = tabs[:4]
    alo, ahi, blo, bhi, ah, bh = coefs
    H = ce.shape[0]
    C = HY_WIDTH
    cb = nb * C
    cols = u.shape[-1]
    blk = pl.BlockSpec((tk, H), lambda c, j: (j, 0))
    coef = pl.BlockSpec((tk, C), lambda c, j: (j, order))
    half = pl.BlockSpec((8, C), lambda c, j: (0, order))
    out = pl.BlockSpec((tk, cb), lambda c, j: (j, c))
    out_h = pl.BlockSpec((8, cb), lambda c, j: (0, c))
    shp = jax.ShapeDtypeStruct((H, cols), BF16)
    shp_h = jax.ShapeDtypeStruct((8, cols), F32)
    return pl.pallas_call(
        functools.partial(_fwd_dft_kernel, nb=nb), grid=(cols // cb, H // tk),
        in_specs=[blk, blk, blk, blk,
                  pl.BlockSpec((2, H, cb), lambda c, j: (0, 0, c), pipeline_mode=pl.Buffered(1)),
                  coef, coef, coef, coef, half, half],
        out_specs=[out, out, out, out, out_h, out_h],
        out_shape=[shp, shp, shp, shp, shp_h, shp_h],
        compiler_params=_params(("parallel", "arbitrary")), name="hy_fwd_dft",
    )(ce, se, co, so, u, alo, ahi, blo, bhi, ah, bh)


def _inv_dft_kernel(ce_ref, se_ref, cot_ref, sot_ref, xa_ref, xb_ref, xc_ref, xd_ref,
                    r1h_ref, r2h_ref, g_ref, z_ref):
    tm = ce_ref.shape[0]
    ye = (jnp.dot(ce_ref[...], xa_ref[...], preferred_element_type=F32)
          + jnp.dot(se_ref[...], xb_ref[...], preferred_element_type=F32))
    yo = (jnp.dot(cot_ref[...], xc_ref[...], preferred_element_type=F32)
          + jnp.dot(sot_ref[...], xd_ref[...], preferred_element_type=F32))
    rows = pl.program_id(1) * tm + lax.broadcasted_iota(jnp.int32, ye.shape, 0)
    odd = (rows & 1) == 1
    r1h = r1h_ref[0:1]
    r2h = r2h_ref[0:1]
    ye = ye + jnp.where(odd, -r1h, r1h)
    yo = yo + jnp.where(odd, -r2h, r2h)
    z_ref[0] = (g_ref[0].astype(F32) * ye).astype(z_ref.dtype)
    z_ref[1] = (g_ref[1].astype(F32) * yo).astype(z_ref.dtype)


def _inv_dft(tabs, xs, gate, out_dtype, *, tm, nb):
    ce, se, _, _, cot, sot = tabs
    xa, xb, xc, xd, r1h, r2h = xs
    H = ce.shape[0]
    cb = nb * HY_WIDTH
    cols = xa.shape[-1]
    blk = pl.BlockSpec((tm, H), lambda c, i: (i, 0))
    full = pl.BlockSpec((H, cb), lambda c, i: (0, c), pipeline_mode=pl.Buffered(1))
    half = pl.BlockSpec((8, cb), lambda c, i: (0, c))
    tile = pl.BlockSpec((2, tm, cb), lambda c, i: (0, i, c))
    return pl.pallas_call(
        _inv_dft_kernel, grid=(cols // cb, H // tm),
        in_specs=[blk, blk, blk, blk, full, full, full, full, half, half, tile],
        out_specs=tile, out_shape=jax.ShapeDtypeStruct((2, H, cols), out_dtype),
        compiler_params=_params(("parallel", "arbitrary")), name="hy_inv_dft",
    )(ce, se, cot, sot, xa, xb, xc, xd, r1h, r2h, gate)


def _hyena_features(L):
    deint = lambda a: jnp.concatenate([a[0::2], a[1::2]], axis=0)
    t_idx = deint(jnp.arange(L, dtype=F32))[:, None]
    t_norm = deint(jnp.linspace(0.0, 1.0, L, dtype=F32))[:, None]
    bands = jnp.linspace(1e-4, HY_BANDS - 1, HY_BANDS, dtype=F32)[None, :]
    ang = 2.0 * math.pi * t_idx * bands / L
    z = jnp.concatenate([t_norm, jnp.cos(ang), jnp.sin(ang)], axis=-1)
    z = jnp.pad(z, ((0, 0), (0, LANE - HY_EMB)))
    deltas = jnp.linspace(math.log(HY_TARGET) / HY_SLOW_DECAY,
                          math.log(HY_TARGET) / HY_FAST_DECAY, HY_WIDTH, dtype=F32)
    decay = jnp.exp(-t_norm * jnp.abs(deltas)[None, :])
    return z, decay


def _mix_ffn_kernel(x_ref, ya_ref, yb_ref, yc_ref, gmix_ref, wout_ref, g2_ref, wg_ref, wu_ref,
                    wd_ref, gf_ref, o_ref, yb_scr, *, chunks, final_norm):
    gmix = gmix_ref[...]
    ca = ya_ref.shape[-1]
    cb = yb_ref.shape[-1]
    half = yb_ref.shape[1]
    for j in range(cb // LANE):
        sl = slice(j * LANE, (j + 1) * LANE)
        yb_scr[j, pl.ds(0, half, stride=2), :] = yb_ref[0, :, sl].astype(F32)
        yb_scr[j, pl.ds(1, half, stride=2), :] = yb_ref[1, :, sl].astype(F32)
    yb = jnp.concatenate([yb_scr[j] for j in range(cb // LANE)], axis=-1)
    ymix = jnp.concatenate([
        _rms(ya_ref[0].astype(F32), gmix[:, :ca]),
        _rms(yb, gmix[:, ca:ca + cb]),
        _rms(yc_ref[0].astype(F32), gmix[:, ca + cb:]),
    ], axis=-1).astype(BF16)
    x = x_ref[0] + jnp.dot(ymix, wout_ref[...], preferred_element_type=F32)
    h2 = _rms(x, g2_ref[...]).astype(BF16)
    acc = x
    for lo, hi in chunks:
        sl = slice(lo, hi)
        gate = jnp.dot(h2, wg_ref[:, sl], preferred_element_type=F32)
        up = jnp.dot(h2, wu_ref[:, sl], preferred_element_type=F32)
        act = (gate * jax.nn.sigmoid(gate) * up).astype(BF16)
        acc = acc + jnp.dot(act, wd_ref[sl, :], preferred_element_type=F32)
    if final_norm:
        acc = _rms(acc, gf_ref[...])
    o_ref[0] = acc


def _mix_ffn(x, ya, yb, yc, gmix, wout, g2, wg, wu, wd, gf, *, tm, final_norm):
    B, S, D = x.shape
    ns = S // tm
    tok = lambda w: pl.BlockSpec((1, tm, w), lambda i: (i // ns, i % ns, 0))
    once = lambda a: pl.BlockSpec(a.shape, lambda i: (0,) * a.ndim,
                                  pipeline_mode=pl.Buffered(1))
    ff = wg.shape[1]
    cut = pl.cdiv(pl.cdiv(ff, MXU_WIDTH), 2) * MXU_WIDTH
    chunks = ((0, cut), (cut, ff)) if cut < ff else ((0, ff),)
    return pl.pallas_call(
        functools.partial(_mix_ffn_kernel, chunks=chunks, final_norm=final_norm),
        grid=(B * ns,),
        in_specs=[tok(D), tok(ya.shape[-1]),
                  pl.BlockSpec((2, tm // 2, HY_WIDTH), lambda i: (0, i % ns, i // ns)),
                  tok(yc.shape[-1]), once(gmix), once(wout), once(g2), once(wg), once(wu),
                  once(wd), once(gf)],
        out_specs=tok(D), out_shape=jax.ShapeDtypeStruct((B, S, D), F32),
        scratch_shapes=[pltpu.VMEM((HY_WIDTH // LANE, tm, LANE), F32)],
        compiler_params=_params(("parallel",)), name="mix_ffn",
    )(x, ya, yb, yc, gmix, wout, g2, wg, wu, wd, gf)


def _rope_tables(S):
    half = MLA_ROPE // 2
    pos = jnp.arange(S, dtype=F32)
    inv = ROPE_THETA ** (-jnp.arange(0, MLA_ROPE, 2, dtype=F32) / MLA_ROPE)
    ang = pos[:, None] * inv[None, :]
    cos, sin = jnp.cos(ang), jnp.sin(ang)
    z_lo = jnp.zeros((S, MLA_NOPE), F32)
    z_hi = jnp.zeros((S, HEAD_PAD - MLA_NOPE - 2 * half), F32)
    ck = jnp.concatenate([z_lo, cos, cos, z_hi], axis=-1)
    sk = jnp.concatenate([z_lo, -sin, sin, z_hi], axis=-1)
    scale = math.log2(math.e) / math.sqrt(MLA_NOPE + MLA_ROPE)
    cq = jnp.concatenate([jnp.ones((S, MLA_NOPE), F32), cos, cos, z_hi], axis=-1) * scale
    sq = sk * scale
    return cq, sq, ck, sk


def _pad_heads(w, n_heads, width, keep):
    K = w.shape[0]
    w = w.reshape(K, n_heads, width)[:, :, :keep]
    w = jnp.pad(w, ((0, 0), (0, 0), (0, HEAD_PAD - keep)))
    return w.reshape(K, n_heads * HEAD_PAD)


def _layer_weights(w_in, w_uq, w_ukv):
    D = w_in.shape[0]
    o = MLA_Q_RANK + MLA_KV_RANK
    w_kpe = w_in[:, o:o + MLA_ROPE]
    zl = jnp.zeros((D, MLA_NOPE), F32)
    zh = jnp.zeros((D, HEAD_PAD - MLA_NOPE - MLA_ROPE), F32)
    half = MLA_ROPE // 2
    kpe_pad = jnp.concatenate([zl, w_kpe, zh], axis=-1)
    wlat = jnp.concatenate([w_in[:, :o], kpe_pad], axis=-1).astype(BF16)
    o2 = o + MLA_ROPE
    why = w_in[:, o2:o2 + 3 * HY_WIDTH].astype(BF16)
    wna = w_in[:, o2 + 3 * HY_WIDTH:].astype(BF16)
    qd = MLA_NOPE + MLA_ROPE
    wq = _pad_heads(w_uq, MLA_HEADS, qd, qd).astype(BF16)
    kvd = MLA_NOPE + MLA_V
    wk = _pad_heads(w_ukv, MLA_HEADS, kvd, MLA_NOPE).astype(BF16)
    wv = w_ukv.reshape(-1, MLA_HEADS, kvd)[:, :, MLA_NOPE:].reshape(-1, MLA_WIDTH)
    wvt = wv.T.astype(BF16)
    return wlat, why, wna, wq, wk, wvt


def _pad2(a, rows, cols):
    return jnp.pad(a, ((0, rows - a.shape[0]), (0, cols - a.shape[1])))


def kernel(x, norm1_g, w_in, mla_q_norm_g, mla_w_uq, mla_kv_norm_g, mla_w_ukv, hy_conv_w, hy_conv_b, hy_filt_w1, hy_filt_b1, hy_filt_freq1, hy_filt_w2, hy_filt_b2, hy_filt_freq2, hy_filt_w3, hy_skip, na_rpb, mix_norm_g, w_out, norm2_g, ffn_w_gate, ffn_w_up, ffn_w_down, final_norm_g):
    B, S, D = x.shape
    depth = w_in.shape[0]
    L = S
    tm = min(512, S)
    tq = min(512, S)
    tkv = min(256, S)
    kv_unroll = max(1, min(8, S // tkv))
    t_dft = min(512, L // 2)
    nb = next(n for n in (2, 1) if B % n == 0)

    cq, sq, ck, sk = _rope_tables(S)
    tabs = _dft_tables(L)
    z_feat, decay = _hyena_features(L)
    row = lambda v: v.reshape(1, -1)

    for l in range(depth):
        wlat, why, wna, wq, wk, wvt = _layer_weights(w_in[l], mla_w_uq[l], mla_w_ukv[l])
        q, k, vt, hy, naq, nak, nav = _inproj(
            x, row(norm1_g[l]), wlat, why, wna, row(mla_q_norm_g[l]), wq,
            row(mla_kv_norm_g[l]), wk, wvt, cq, sq, ck, sk, tm=tm)

        y_a = _mla_attn(q, k, vt, tq=tq, tk=tkv, unroll=kv_unroll)
        y_c = _natten(naq, nak, nav, _natten_bias_pairs(na_rpb[l]), G=min(16, S // GRID_W))

        u, g1, g2 = _short_conv(hy, hy_conv_w[l], row(hy_conv_b[l]), B)
        hp, hm = _filters(
            z_feat, _pad2(hy_filt_w1[l], LANE, LANE), _pad2(row(hy_filt_b1[l]), 1, LANE),
            _pad2(row(hy_filt_freq1[l]), 1, LANE), _pad2(hy_filt_w2[l], LANE, LANE),
            _pad2(row(hy_filt_b2[l]), 1, LANE), _pad2(row(hy_filt_freq2[l]), 1, LANE),
            _pad2(hy_filt_w3[l], LANE, hy_filt_w3.shape[-1]), decay, tm=tm)
        halves = lambda a: a.reshape(2, L // 2, a.shape[-1])
        coefs = _coefs(tabs, halves(hp), halves(hm), row(hy_skip[l]), tk=t_dft)
        xs = _fwd_dft(tabs, u, coefs, 0, tk=t_dft, nb=nb)
        z1 = _inv_dft(tabs, xs, g1, BF16, tm=t_dft, nb=nb)
        xs = _fwd_dft(tabs, z1, coefs, 1, tk=t_dft, nb=nb)
        y_b = _inv_dft(tabs, xs, g2, BF16, tm=t_dft, nb=nb)

        x = _mix_ffn(x, y_a, y_b, y_c, row(mix_norm_g[l]), w_out[l].astype(BF16),
                     row(norm2_g[l]), ffn_w_gate[l].astype(BF16), ffn_w_up[l].astype(BF16),
                     ffn_w_down[l].astype(BF16), row(final_norm_g),
                     tm=tm, final_norm=(l == depth - 1))
    return x
```

```python
import functools
import math

import jax
import jax.numpy as jnp
import numpy as np
from jax import lax
from jax.experimental import pallas as pl
from jax.experimental.pallas import tpu as pltpu

F32 = jnp.float32
BF16 = jnp.bfloat16

NORM_EPS = 1e-6
MLA_HEADS = 6
MLA_NOPE = 64
MLA_ROPE = 32
MLA_V = 64
MLA_Q_RANK = 256
MLA_KV_RANK = 128
ROPE_THETA = 10000.0
HY_WIDTH = 384
HY_ORDER = 2
HY_BANDS = 8
HY_EMB = 1 + 2 * HY_BANDS
HY_FFN = 64
HY_FAST_DECAY = 0.3
HY_SLOW_DECAY = 1.5
HY_TARGET = 1e-2
NA_HEADS = 4
NA_HEAD_DIM = 64
NA_WIDTH = NA_HEADS * NA_HEAD_DIM
GRID_W = 64
NA_KH = 8
NA_KW = 16
MLA_WIDTH = MLA_HEADS * MLA_V

LANE = 128
HEAD_PAD = 128
F32_SUBLANES = 8
BF16_SUBLANES = 16
MLA_VPAD = MLA_V + BF16_SUBLANES
VMEM_LIMIT = 56 * 1024 * 1024
MASK_VALUE = -1e30
MXU_WIDTH = 256
DFT_COL_CHUNK = MXU_WIDTH


def _params(sem, vmem=VMEM_LIMIT):
    return pltpu.CompilerParams(dimension_semantics=sem, vmem_limit_bytes=vmem)


def _rms(x, g):
    return x * lax.rsqrt(jnp.mean(x * x, axis=-1, keepdims=True) + NORM_EPS) * g


def _const_spec(shape):
    nd = len(shape)
    return pl.BlockSpec(shape, lambda *_: (0,) * nd)


def _dft_table_kernel(ce_ref, se_ref, co_ref, so_ref, cot_ref, sot_ref,
                      eec_ref, ees_ref, eoc_ref, eos_ref, *, L, tk):
    i = pl.program_id(0)
    n = 2 * L
    H = L // 2
    w = 2.0 * math.pi / n

    def angle(prod):
        return (prod & (n - 1)).astype(F32) * w

    @pl.when(i == 0)
    def _():
        r = lax.broadcasted_iota(jnp.int32, (tk, H), 0)
        c = lax.broadcasted_iota(jnp.int32, (tk, H), 1)
        ang_e = angle(r * (2 * c))
        ang_o = angle(r * (2 * c + 1))
        eec_ref[...] = jnp.cos(ang_e)
        ees_ref[...] = jnp.sin(ang_e)
        eoc_ref[...] = jnp.cos(ang_o)
        eos_ref[...] = jnp.sin(ang_o)

    k0 = i * tk
    c1 = lax.broadcasted_iota(jnp.int32, (1, H), 1)

    def rotate(a0, ec_ref, es_ref):
        ca, sa = jnp.cos(a0), jnp.sin(a0)
        ec, es = ec_ref[...], es_ref[...]
        return (ca * ec - sa * es).astype(BF16), (sa * ec + ca * es).astype(BF16)

    ce_ref[...], se_ref[...] = rotate(angle(k0 * (2 * c1)), eec_ref, ees_ref)
    co_ref[...], so_ref[...] = rotate(angle(k0 * (2 * c1 + 1)), eoc_ref, eos_ref)
    cot_ref[...], sot_ref[...] = rotate(angle(c1 * (2 * k0 + 1)), eec_ref, ees_ref)


def _dft_tables(L):
    assert L & (L - 1) == 0, "sequence length must be a power of two"
    H = L // 2
    tk = min(128, H)
    spec = pl.BlockSpec((tk, H), lambda i: (i, 0))
    shp = jax.ShapeDtypeStruct((H, H), BF16)
    return pl.pallas_call(
        functools.partial(_dft_table_kernel, L=L, tk=tk),
        grid=(H // tk,),
        out_specs=[spec] * 6,
        out_shape=[shp] * 6,
        scratch_shapes=[pltpu.VMEM((tk, H), F32)] * 4,
        compiler_params=_params(("arbitrary",)),
        name="dft_tables",
    )()


def _swap_rope_halves(x):
    half = MLA_ROPE // 2
    lane = lax.broadcasted_iota(jnp.int32, (1, x.shape[1]), 1) % HEAD_PAD
    first = (lane >= MLA_NOPE) & (lane < MLA_NOPE + half)
    width = x.shape[1]
    return jnp.where(first, pltpu.roll(x, width - half, axis=1), pltpu.roll(x, half, axis=1))


def _inproj_kernel(x_ref, g1_ref, wlat_ref, why_ref, wna_ref, gq_ref, wq_ref,
                   gkv_ref, wk_ref, wvt_ref, cq_ref, sq_ref, ck_ref, sk_ref,
                   q_ref, k_ref, vt_ref, hy_ref, naq_ref, nak_ref, nav_ref, hy_scr):
    x = x_ref[0]
    h = _rms(x, g1_ref[...]).astype(BF16)
    lat = jnp.dot(h, wlat_ref[...], preferred_element_type=F32)
    hy = jnp.dot(h, why_ref[...], preferred_element_type=F32)
    half = hy.shape[0] // 2
    for j in range(hy.shape[1] // LANE):
        sl = slice(j * LANE, (j + 1) * LANE)
        hy_scr[j] = hy[:, sl]
        hy_ref[0, :, sl] = hy_scr[j, pl.ds(0, half, stride=2), :].astype(BF16)
        hy_ref[1, :, sl] = hy_scr[j, pl.ds(1, half, stride=2), :].astype(BF16)
    na = jnp.dot(h, wna_ref[...], preferred_element_type=F32)
    naq_ref[0] = (na[:, :NA_WIDTH] * (1.0 / math.sqrt(NA_HEAD_DIM))).astype(BF16)
    nak_ref[0] = na[:, NA_WIDTH:2 * NA_WIDTH].astype(BF16)
    nav_ref[0] = na[:, 2 * NA_WIDTH:].astype(BF16)

    c_q = lat[:, :MLA_Q_RANK]
    c_kv = lat[:, MLA_Q_RANK:MLA_Q_RANK + MLA_KV_RANK]
    o = MLA_Q_RANK + MLA_KV_RANK
    kpe = lat[:, o:o + HEAD_PAD]

    cqn = _rms(c_q, gq_ref[...]).astype(BF16)
    qf = jnp.dot(cqn, wq_ref[...], preferred_element_type=F32)
    qs = _swap_rope_halves(qf)
    cq = cq_ref[...]
    sq = sq_ref[...]
    ckvn = _rms(c_kv, gkv_ref[...])
    kf = jnp.dot(ckvn.astype(BF16), wk_ref[...], preferred_element_type=F32)
    kpe_r = kpe * ck_ref[...] + _swap_rope_halves(kpe) * sk_ref[...]
    for hd in range(MLA_HEADS):
        sl = slice(hd * HEAD_PAD, (hd + 1) * HEAD_PAD)
        q_ref[0, :, sl] = (qf[:, sl] * cq + qs[:, sl] * sq).astype(BF16)
        k_ref[0, :, sl] = (kf[:, sl] + kpe_r).astype(BF16)
    vt = jnp.dot(wvt_ref[...], ckvn.T.astype(BF16), preferred_element_type=F32).astype(BF16)
    ones = jnp.ones((MLA_VPAD - MLA_V, vt.shape[1]), BF16)
    for hd in range(MLA_HEADS):
        vt_ref[0, hd * MLA_VPAD:hd * MLA_VPAD + MLA_V, :] = vt[hd * MLA_V:(hd + 1) * MLA_V]
        vt_ref[0, hd * MLA_VPAD + MLA_V:(hd + 1) * MLA_VPAD, :] = ones


def _inproj(x, g1, wlat, why, wna, gq, wq, gkv, wk, wvt, cq, sq, ck, sk, *, tm):
    B, S, D = x.shape
    ns = S // tm
    hyw = why.shape[1]
    qw = MLA_HEADS * HEAD_PAD
    tok = lambda w: pl.BlockSpec((1, tm, w), lambda i: (i // ns, i % ns, 0))
    tab = pl.BlockSpec((tm, HEAD_PAD), lambda i: (i % ns, 0))
    in_specs = [tok(D), _const_spec(g1.shape), _const_spec(wlat.shape), _const_spec(why.shape),
                _const_spec(wna.shape), _const_spec(gq.shape), _const_spec(wq.shape),
                _const_spec(gkv.shape), _const_spec(wk.shape),
                _const_spec(wvt.shape), tab, tab, tab, tab]
    out_specs = [tok(qw), tok(qw),
                 pl.BlockSpec((1, MLA_HEADS * MLA_VPAD, tm), lambda i: (i // ns, 0, i % ns)),
                 pl.BlockSpec((2, tm // 2, hyw), lambda i: (0, i % ns, i // ns)),
                 tok(NA_WIDTH), tok(NA_WIDTH), tok(NA_WIDTH)]
    out_shape = [jax.ShapeDtypeStruct((B, S, qw), BF16),
                 jax.ShapeDtypeStruct((B, S, qw), BF16),
                 jax.ShapeDtypeStruct((B, MLA_HEADS * MLA_VPAD, S), BF16),
                 jax.ShapeDtypeStruct((2, S // 2, B * hyw), BF16),
                 jax.ShapeDtypeStruct((B, S, NA_WIDTH), BF16),
                 jax.ShapeDtypeStruct((B, S, NA_WIDTH), BF16),
                 jax.ShapeDtypeStruct((B, S, NA_WIDTH), BF16)]
    return pl.pallas_call(
        _inproj_kernel, grid=(B * ns,), in_specs=in_specs, out_specs=out_specs,
        out_shape=out_shape, scratch_shapes=[pltpu.VMEM((hyw // LANE, tm, LANE), F32)],
        compiler_params=_params(("parallel",)), name="inproj",
    )(x, g1, wlat, why, wna, gq, wq, gkv, wk, wvt, cq, sq, ck, sk)


def _mla_attn_kernel(q_ref, k_ref, vt_ref, o_ref, m_ref, acc_ref, *, tk, unroll):
    S = k_ref.shape[1]
    nk = S // tk
    m_ref[...] = jnp.full(m_ref.shape, -jnp.inf, F32)
    acc_ref[...] = jnp.zeros(acc_ref.shape, F32)

    def body(j, carry):
        base = j * (unroll * tk)

        def scores(c, hd):
            off = pl.multiple_of(base + c * tk, tk)
            sl = slice(hd * HEAD_PAD, (hd + 1) * HEAD_PAD)
            kj = k_ref[0, pl.ds(off, tk), sl]
            s = lax.dot_general(kj, q_ref[0, :, sl], (((1,), (1,)), ((), ())),
                                preferred_element_type=F32)
            m_old = m_ref[hd]
            m_new = jnp.maximum(m_old, jnp.max(s, axis=0, keepdims=True))
            m_ref[hd] = m_new
            return s, m_old, m_new

        def probs(s, m_old, m_new):
            return jnp.exp2(s - m_new).astype(BF16), jnp.exp2(m_old - m_new)

        def accumulate(c, hd, p, alpha):
            off = pl.multiple_of(base + c * tk, tk)
            vj = vt_ref[0, hd * MLA_VPAD:(hd + 1) * MLA_VPAD, pl.ds(off, tk)]
            acc_ref[hd] = alpha * acc_ref[hd] + jnp.dot(vj, p, preferred_element_type=F32)

        items = [(c, hd) for c in range(unroll) for hd in range(MLA_HEADS)]
        n = len(items)
        st_scores, st_probs = {}, {}
        for t in range(n + 2):
            if t < n:
                st_scores[t] = scores(*items[t])
            if 0 <= t - 1 < n:
                st_probs[t - 1] = probs(*st_scores.pop(t - 1))
            if 0 <= t - 2 < n:
                accumulate(*items[t - 2], *st_probs.pop(t - 2))
        return carry

    lax.fori_loop(0, nk // unroll, body, 0)
    outs = []
    for hd in range(MLA_HEADS):
        acc = acc_ref[hd]
        outs.append(acc[:MLA_V] / acc[MLA_V:MLA_V + 1])
    o_ref[0] = jnp.concatenate(outs, axis=0).T.astype(o_ref.dtype)


def _mla_attn(q, k, vt, *, tq, tk, unroll):
    B, S, qw = q.shape
    vrows = vt.shape[1]
    assert S % (tk * unroll) == 0
    return pl.pallas_call(
        functools.partial(_mla_attn_kernel, tk=tk, unroll=unroll),
        grid=(B, S // tq),
        in_specs=[pl.BlockSpec((1, tq, qw), lambda b, i: (b, i, 0)),
                  pl.BlockSpec((1, S, qw), lambda b, i: (b, 0, 0)),
                  pl.BlockSpec((1, vrows, S), lambda b, i: (b, 0, 0))],
        out_specs=pl.BlockSpec((1, tq, MLA_WIDTH), lambda b, i: (b, i, 0)),
        out_shape=jax.ShapeDtypeStruct((B, S, MLA_WIDTH), BF16),
        scratch_shapes=[pltpu.VMEM((MLA_HEADS, 1, tq), F32),
                        pltpu.VMEM((MLA_HEADS, MLA_VPAD, tq), F32)],
        compiler_params=_params(("parallel", "arbitrary")), name="mla_attn",
    )(q, k, vt)


def _natten_kernel(q_ref, k_ref, v_ref, bias_ref, o_ref, *, R, KH, G):
    col_head = lax.broadcasted_iota(jnp.int32, (1, NA_WIDTH), 1) // NA_HEAD_DIM
    sels = [col_head == hd for hd in range(NA_HEADS)]

    def window(i):
        r = pl.program_id(1) * G + i
        start = jnp.clip(r - KH // 2, 0, R - KH)
        return start - r + (NA_KH - 1), pl.multiple_of(start * GRID_W, GRID_W)

    def scores(i):
        dr0, off = window(i)
        q = q_ref[0, i * GRID_W:(i + 1) * GRID_W, :]
        qm = jnp.concatenate([jnp.where(sel, q, jnp.zeros_like(q)) for sel in sels], axis=0)
        kw = k_ref[0, pl.ds(off, KH * GRID_W), :]
        s = lax.dot_general(qm, kw, (((1,), (1,)), ((), ())),
                            preferred_element_type=F32)
        bias = jnp.concatenate([bias_ref[dr0 + kh] for kh in range(0, KH, 2)], axis=-1)
        logits = s + bias
        return logits, jnp.max(logits, axis=-1, keepdims=True)

    def probs(logits, m):
        p = jnp.exp(logits - m)
        return p.astype(BF16), jnp.sum(p, axis=-1, keepdims=True)

    def output(i, p, l):
        _, off = window(i)
        vw = v_ref[0, pl.ds(off, KH * GRID_W), :]
        o = jnp.dot(p, vw, preferred_element_type=F32) / l
        y = jnp.zeros((GRID_W, NA_WIDTH), F32)
        for hd, sel in enumerate(sels):
            y = y + jnp.where(sel, o[hd * GRID_W:(hd + 1) * GRID_W], 0.0)
        o_ref[0, i * GRID_W:(i + 1) * GRID_W, :] = y.astype(o_ref.dtype)

    st_scores, st_probs = {}, {}
    for t in range(G + 2):
        if t < G:
            st_scores[t] = scores(t)
        if 0 <= t - 1 < G:
            st_probs[t - 1] = probs(*st_scores.pop(t - 1))
        if 0 <= t - 2 < G:
            output(t - 2, *st_probs.pop(t - 2))


def _natten(q, k, v, bias_pairs, *, G):
    B, S, _ = q.shape
    R = S // GRID_W
    KH = min(NA_KH, R)
    assert KH % 2 == 0 and R % G == 0
    return pl.pallas_call(
        functools.partial(_natten_kernel, R=R, KH=KH, G=G),
        grid=(B, R // G),
        in_specs=[pl.BlockSpec((1, G * GRID_W, NA_WIDTH), lambda b, r: (b, r, 0)),
                  pl.BlockSpec((1, S, NA_WIDTH), lambda b, r: (b, 0, 0)),
                  pl.BlockSpec((1, S, NA_WIDTH), lambda b, r: (b, 0, 0)),
                  _const_spec(bias_pairs.shape)],
        out_specs=pl.BlockSpec((1, G * GRID_W, NA_WIDTH), lambda b, r: (b, r, 0)),
        out_shape=jax.ShapeDtypeStruct((B, S, NA_WIDTH), BF16),
        compiler_params=_params(("parallel", "arbitrary")), name="natten",
    )(q, k, v, bias_pairs)


def _natten_bias_pairs(rpb):
    c = np.arange(GRID_W)
    start = np.clip(c - NA_KW // 2, 0, GRID_W - NA_KW)
    v = c[None, :]
    inwin = (v >= start[:, None]) & (v < start[:, None] + NA_KW)
    dc = v - c[:, None] + (NA_KW - 1)
    onehot = (dc[:, :, None] == np.arange(2 * NA_KW - 1)[None, None, :]) & inwin[:, :, None]
    t = jnp.einsum('hdj,wvj->hdwv', rpb.astype(F32), jnp.asarray(onehot, F32),
                   precision=lax.Precision.HIGHEST)
    t = jnp.where(jnp.asarray(inwin)[None, None], t, MASK_VALUE)
    t = jnp.concatenate([t[:, :-1], t[:, 1:]], axis=-1).astype(F32)
    return t.transpose(1, 0, 2, 3).reshape(t.shape[1], NA_HEADS * GRID_W, 2 * GRID_W)


def _alt_sign_sum(x):
    rows, cols = x.shape
    if rows % F32_SUBLANES == 0:
        x = jnp.sum(x.astype(F32).reshape(rows // F32_SUBLANES, F32_SUBLANES, cols), axis=0)
    odd = (lax.broadcasted_iota(jnp.int32, x.shape, 0) & 1) == 1
    xf = x.astype(F32)
    return jnp.sum(jnp.where(odd, -xf, xf), axis=0, keepdims=True)


def _short_conv_kernel(v_ref, x1_ref, x2_ref, wv_ref, w1_ref, w2_ref, bv_ref, b1_ref, b2_ref,
                       u_ref, g1_ref, g2_ref):
    H = v_ref.shape[1]
    i = lax.broadcasted_iota(jnp.int32, v_ref.shape[1:], 0)

    def conv(x_ref, w_ref, b_ref, o_ref):
        xe = x_ref[0].astype(F32)
        xo = x_ref[1].astype(F32)
        xo_prev = jnp.where(i == 0, 0.0, pltpu.roll(xo, 1, axis=0))
        xe_next = jnp.where(i == H - 1, 0.0, pltpu.roll(xe, H - 1, axis=0))
        w = w_ref[...]
        b = b_ref[...]
        o_ref[0] = (w[0:1] * xo_prev + w[1:2] * xe + w[2:3] * xo + b).astype(o_ref.dtype)
        o_ref[1] = (w[0:1] * xe + w[1:2] * xo + w[2:3] * xe_next + b).astype(o_ref.dtype)

    conv(v_ref, wv_ref, bv_ref, u_ref)
    conv(x1_ref, w1_ref, b1_ref, g1_ref)
    conv(x2_ref, w2_ref, b2_ref, g2_ref)


def _short_conv(hy, conv_w, conv_b, B):
    _, H, _ = hy.shape
    C = HY_WIDTH
    nj = C // LANE
    per_b = 3 * nj
    seg = lambda s: pl.BlockSpec((2, H, LANE), lambda b, j: (0, 0, b * per_b + s * nj + j))
    wseg = lambda s: pl.BlockSpec((3, LANE), lambda b, j: (0, s * nj + j))
    bseg = lambda s: pl.BlockSpec((1, LANE), lambda b, j: (0, s * nj + j))
    out = pl.BlockSpec((2, H, LANE), lambda b, j: (0, 0, b * nj + j))
    return pl.pallas_call(
        _short_conv_kernel, grid=(B, nj),
        in_specs=[seg(0), seg(1), seg(2), wseg(0), wseg(1), wseg(2), bseg(0), bseg(1), bseg(2)],
        out_specs=[out, out, out],
        out_shape=[jax.ShapeDtypeStruct((2, H, B * C), BF16)] * 3,
        compiler_params=_params(("parallel", "parallel")), name="hy_short_conv",
    )(hy, hy, hy, conv_w, conv_w, conv_w, conv_b, conv_b, conv_b)


def _filter_kernel(z_ref, w1_ref, b1_ref, f1_ref, w2_ref, b2_ref, f2_ref, w3_ref, dec_ref,
                   hp_ref, hm_ref):
    hi = lax.Precision.HIGHEST
    dot = lambda a, b: jnp.dot(a, b, precision=hi, preferred_element_type=F32)
    h = jnp.sin(f1_ref[...] * (dot(z_ref[...], w1_ref[...]) + b1_ref[...]))
    h = jnp.sin(f2_ref[...] * (dot(h, w2_ref[...]) + b2_ref[...]))
    h = jnp.dot(h.astype(BF16), w3_ref[...].astype(BF16),
                preferred_element_type=F32)
    dec = dec_ref[...]
    C = HY_WIDTH
    for o in range(HY_ORDER):
        hf = h[:, (2 * o) * C:(2 * o + 1) * C] * dec
        hb = h[:, (2 * o + 1) * C:(2 * o + 2) * C] * dec
        hp_ref[:, o * C:(o + 1) * C] = (hf + hb).astype(BF16)
        hm_ref[:, o * C:(o + 1) * C] = (hf - hb).astype(BF16)


def _filters(z, w1, b1, f1, w2, b2, f2, w3, decay, *, tm):
    L = z.shape[0]
    OC = HY_ORDER * HY_WIDTH
    row = lambda w: pl.BlockSpec((tm, w), lambda i: (i, 0))
    return pl.pallas_call(
        _filter_kernel, grid=(L // tm,),
        in_specs=[row(z.shape[1]), _const_spec(w1.shape), _const_spec(b1.shape),
                  _const_spec(f1.shape), _const_spec(w2.shape), _const_spec(b2.shape),
                  _const_spec(f2.shape), _const_spec(w3.shape), row(HY_WIDTH)],
        out_specs=[row(OC), row(OC)],
        out_shape=[jax.ShapeDtypeStruct((L, OC), BF16), jax.ShapeDtypeStruct((L, OC), BF16)],
        compiler_params=_params(("parallel",)), name="hy_filters",
    )(z, w1, b1, f1, w2, b2, f2, w3, decay)


def _coef_kernel(ce_ref, se_ref, co_ref, so_ref, hp_ref, hm_ref, skip_ref,
                 alo_ref, ahi_ref, blo_ref, bhi_ref, ah_ref, bh_ref, *, L):
    i = pl.program_id(0)
    tk = ce_ref.shape[0]
    inv = 1.0 / L
    skip = skip_ref[...]
    pe = jnp.dot(ce_ref[...], hp_ref[0], preferred_element_type=F32)
    po = jnp.dot(co_ref[...], hp_ref[1], preferred_element_type=F32)
    qe = jnp.dot(se_ref[...], hm_ref[0], preferred_element_type=F32)
    qo = jnp.dot(so_ref[...], hm_ref[1], preferred_element_type=F32)
    is0 = (i * tk + lax.broadcasted_iota(jnp.int32, pe.shape, 0)) == 0
    sc = jnp.where(is0, 0.5 * inv, inv)
    alo_ref[...] = sc * (pe + po + skip)
    ahi_ref[...] = sc * (pe - po + skip)
    blo_ref[...] = -inv * (qe + qo)
    bhi_ref[...] = inv * (qe - qo)

    @pl.when(i == 0)
    def _():
        gr = _alt_sign_sum(hp_ref[0]) + skip
        gi = -_alt_sign_sum(hm_ref[1])
        ah_ref[...] = jnp.broadcast_to(inv * gr, ah_ref.shape)
        bh_ref[...] = jnp.broadcast_to(inv * gi, bh_ref.shape)


def _coefs(tabs, hp, hm, skip, *, tk):
    ce, se, co, so = tabs[:4]
    H = ce.shape[0]
    OC = hp.shape[-1]
    blk = pl.BlockSpec((tk, H), lambda i: (i, 0))
    out = pl.BlockSpec((tk, OC), lambda i: (i, 0))
    half = pl.BlockSpec((F32_SUBLANES, OC), lambda i: (0, 0))
    shp = jax.ShapeDtypeStruct((H, OC), F32)
    shp_h = jax.ShapeDtypeStruct((F32_SUBLANES, OC), F32)
    return pl.pallas_call(
        functools.partial(_coef_kernel, L=2 * H), grid=(H // tk,),
        in_specs=[blk, blk, blk, blk, _const_spec(hp.shape), _const_spec(hm.shape),
                  _const_spec(skip.shape)],
        out_specs=[out, out, out, out, half, half],
        out_shape=[shp, shp, shp, shp, shp_h, shp_h],
        compiler_params=_params(("arbitrary",)), name="hy_coefs",
    )(ce, se, co, so, hp, hm, skip)


def _fwd_dft_kernel(ce_ref, se_ref, co_ref, so_ref, u_ref, alo_ref, ahi_ref, blo_ref, bhi_ref,
                    ah_ref, bh_ref, xa_ref, xb_ref, xc_ref, xd_ref, r1h_ref, r2h_ref, *, nb):
    tile = lambda x: jnp.concatenate([x] * nb, axis=-1) if nb > 1 else x
    a_lo_t, a_hi_t = tile(alo_ref[...]), tile(ahi_ref[...])
    b_lo_t, b_hi_t = tile(blo_ref[...]), tile(bhi_ref[...])
    cb = u_ref.shape[-1]
    chunk = DFT_COL_CHUNK if cb % DFT_COL_CHUNK == 0 else LANE
    for c in range(cb // chunk):
        sl = slice(c * chunk, (c + 1) * chunk)
        ue = u_ref[0, :, sl]
        uo = u_ref[1, :, sl]
        pe = jnp.dot(ce_ref[...], ue, preferred_element_type=F32)
        po = jnp.dot(co_ref[...], uo, preferred_element_type=F32)
        qe = jnp.dot(se_ref[...], ue, preferred_element_type=F32)
        qo = jnp.dot(so_ref[...], uo, preferred_element_type=F32)
        p_lo, p_hi, q_lo, q_hi = pe + po, pe - po, qe + qo, qo - qe
        a_lo, a_hi, b_lo, b_hi = a_lo_t[:, sl], a_hi_t[:, sl], b_lo_t[:, sl], b_hi_t[:, sl]
        r1_lo = p_lo * a_lo + q_lo * b_lo
        r2_lo = q_lo * a_lo - p_lo * b_lo
        r1_hi = p_hi * a_hi + q_hi * b_hi
        r2_hi = q_hi * a_hi - p_hi * b_hi
        xa_ref[:, sl] = (r1_lo + r1_hi).astype(BF16)
        xb_ref[:, sl] = (r2_lo - r2_hi).astype(BF16)
        xc_ref[:, sl] = (r1_lo - r1_hi).astype(BF16)
        xd_ref[:, sl] = (r2_lo + r2_hi).astype(BF16)

    @pl.when(pl.program_id(1) == 0)
    def _():
        ph = _alt_sign_sum(u_ref[0])
        qh = _alt_sign_sum(u_ref[1])
        a_h = tile(ah_ref[0:1])
        b_h = tile(bh_ref[0:1])
        r1h_ref[...] = jnp.broadcast_to(ph * a_h + qh * b_h, r1h_ref.shape)
        r2h_ref[...] = jnp.broadcast_to(qh * a_h - ph * b_h, r2h_ref.shape)


def _fwd_dft(tabs, u, coefs, order, *, tk, nb):
    ce, se, co, so = tabs[:4]
    alo, ahi, blo, bhi, ah, bh = coefs
    H = ce.shape[0]
    C = HY_WIDTH
    cb = nb * C
    cols = u.shape[-1]
    blk = pl.BlockSpec((tk, H), lambda c, j: (j, 0))
    coef = pl.BlockSpec((tk, C), lambda c, j: (j, order))
    half = pl.BlockSpec((F32_SUBLANES, C), lambda c, j: (0, order))
    out = pl.BlockSpec((tk, cb), lambda c, j: (j, c))
    out_h = pl.BlockSpec((F32_SUBLANES, cb), lambda c, j: (0, c))
    shp = jax.ShapeDtypeStruct((H, cols), BF16)
    shp_h = jax.ShapeDtypeStruct((F32_SUBLANES, cols), F32)
    return pl.pallas_call(
        functools.partial(_fwd_dft_kernel, nb=nb), grid=(cols // cb, H // tk),
        in_specs=[blk, blk, blk, blk,
                  pl.BlockSpec((2, H, cb), lambda c, j: (0, 0, c), pipeline_mode=pl.Buffered(1)),
                  coef, coef, coef, coef, half, half],
        out_specs=[out, out, out, out, out_h, out_h],
        out_shape=[shp, shp, shp, shp, shp_h, shp_h],
        compiler_params=_params(("parallel", "arbitrary")), name="hy_fwd_dft",
    )(ce, se, co, so, u, alo, ahi, blo, bhi, ah, bh)


def _inv_dft_kernel(ce_ref, se_ref, cot_ref, sot_ref, xa_ref, xb_ref, xc_ref, xd_ref,
                    r1h_ref, r2h_ref, g_ref, z_ref):
    tm = ce_ref.shape[0]
    ye = (jnp.dot(ce_ref[...], xa_ref[...], preferred_element_type=F32)
          + jnp.dot(se_ref[...], xb_ref[...], preferred_element_type=F32))
    yo = (jnp.dot(cot_ref[...], xc_ref[...], preferred_element_type=F32)
          + jnp.dot(sot_ref[...], xd_ref[...], preferred_element_type=F32))
    rows = pl.program_id(1) * tm + lax.broadcasted_iota(jnp.int32, ye.shape, 0)
    odd = (rows & 1) == 1
    r1h = r1h_ref[0:1]
    r2h = r2h_ref[0:1]
    ye = ye + jnp.where(odd, -r1h, r1h)
    yo = yo + jnp.where(odd, -r2h, r2h)
    z_ref[0] = (g_ref[0].astype(F32) * ye).astype(z_ref.dtype)
    z_ref[1] = (g_ref[1].astype(F32) * yo).astype(z_ref.dtype)


def _inv_dft(tabs, xs, gate, out_dtype, *, tm, nb):
    ce, se, _, _, cot, sot = tabs
    xa, xb, xc, xd, r1h, r2h = xs
    H = ce.shape[0]
    cb = nb * HY_WIDTH
    cols = xa.shape[-1]
    blk = pl.BlockSpec((tm, H), lambda c, i: (i, 0))
    full = pl.BlockSpec((H, cb), lambda c, i: (0, c), pipeline_mode=pl.Buffered(1))
    half = pl.BlockSpec((F32_SUBLANES, cb), lambda c, i: (0, c))
    tile = pl.BlockSpec((2, tm, cb), lambda c, i: (0, i, c))
    return pl.pallas_call(
        _inv_dft_kernel, grid=(cols // cb, H // tm),
        in_specs=[blk, blk, blk, blk, full, full, full, full, half, half, tile],
        out_specs=tile, out_shape=jax.ShapeDtypeStruct((2, H, cols), out_dtype),
        compiler_params=_params(("parallel", "arbitrary")), name="hy_inv_dft",
    )(ce, se, cot, sot, xa, xb, xc, xd, r1h, r2h, gate)


def _hyena_features(L):
    deint = lambda a: jnp.concatenate([a[0::2], a[1::2]], axis=0)
    t_idx = deint(jnp.arange(L, dtype=F32))[:, None]
    t_norm = deint(jnp.linspace(0.0, 1.0, L, dtype=F32))[:, None]
    bands = jnp.linspace(1e-4, HY_BANDS - 1, HY_BANDS, dtype=F32)[None, :]
    ang = 2.0 * math.pi * t_idx * bands / L
    z = jnp.concatenate([t_norm, jnp.cos(ang), jnp.sin(ang)], axis=-1)
    z = jnp.pad(z, ((0, 0), (0, LANE - HY_EMB)))
    deltas = jnp.linspace(math.log(HY_TARGET) / HY_SLOW_DECAY,
                          math.log(HY_TARGET) / HY_FAST_DECAY, HY_WIDTH, dtype=F32)
    decay = jnp.exp(-t_norm * jnp.abs(deltas)[None, :])
    return z, decay


def _mix_ffn_kernel(x_ref, ya_ref, yb_ref, yc_ref, gmix_ref, wout_ref, g2_ref, wg_ref, wu_ref,
                    wd_ref, gf_ref, o_ref, yb_scr, *, chunks, final_norm):
    gmix = gmix_ref[...]
    ca = ya_ref.shape[-1]
    cb = yb_ref.shape[-1]
    half = yb_ref.shape[1]
    for j in range(cb // LANE):
        sl = slice(j * LANE, (j + 1) * LANE)
        yb_scr[j, pl.ds(0, half, stride=2), :] = yb_ref[0, :, sl].astype(F32)
        yb_scr[j, pl.ds(1, half, stride=2), :] = yb_ref[1, :, sl].astype(F32)
    yb = jnp.concatenate([yb_scr[j] for j in range(cb // LANE)], axis=-1)
    ymix = jnp.concatenate([
        _rms(ya_ref[0].astype(F32), gmix[:, :ca]),
        _rms(yb, gmix[:, ca:ca + cb]),
        _rms(yc_ref[0].astype(F32), gmix[:, ca + cb:]),
    ], axis=-1).astype(BF16)
    x = x_ref[0] + jnp.dot(ymix, wout_ref[...], preferred_element_type=F32)
    h2 = _rms(x, g2_ref[...]).astype(BF16)
    acc = x
    for lo, hi in chunks:
        sl = slice(lo, hi)
        gate = jnp.dot(h2, wg_ref[:, sl], preferred_element_type=F32)
        up = jnp.dot(h2, wu_ref[:, sl], preferred_element_type=F32)
        act = (gate * jax.nn.sigmoid(gate) * up).astype(BF16)
        acc = acc + jnp.dot(act, wd_ref[sl, :], preferred_element_type=F32)
    if final_norm:
        acc = _rms(acc, gf_ref[...])
    o_ref[0] = acc


def _mix_ffn(x, ya, yb, yc, gmix, wout, g2, wg, wu, wd, gf, *, tm, final_norm):
    B, S, D = x.shape
    ns = S // tm
    tok = lambda w: pl.BlockSpec((1, tm, w), lambda i: (i // ns, i % ns, 0))
    once = lambda a: pl.BlockSpec(a.shape, lambda i: (0,) * a.ndim,
                                  pipeline_mode=pl.Buffered(1))
    ff = wg.shape[1]
    cut = pl.cdiv(pl.cdiv(ff, MXU_WIDTH), 2) * MXU_WIDTH
    chunks = ((0, cut), (cut, ff)) if cut < ff else ((0, ff),)
    return pl.pallas_call(
        functools.partial(_mix_ffn_kernel, chunks=chunks, final_norm=final_norm),
        grid=(B * ns,),
        in_specs=[tok(D), tok(ya.shape[-1]),
                  pl.BlockSpec((2, tm // 2, HY_WIDTH), lambda i: (0, i % ns, i // ns)),
                  tok(yc.shape[-1]), once(gmix), once(wout), once(g2), once(wg), once(wu),
                  once(wd), once(gf)],
        out_specs=tok(D), out_shape=jax.ShapeDtypeStruct((B, S, D), F32),
        scratch_shapes=[pltpu.VMEM((HY_WIDTH // LANE, tm, LANE), F32)],
        compiler_params=_params(("parallel",)), name="mix_ffn",
    )(x, ya, yb, yc, gmix, wout, g2, wg, wu, wd, gf)


def _rope_tables(S):
    half = MLA_ROPE // 2
    pos = jnp.arange(S, dtype=F32)
    inv = ROPE_THETA ** (-jnp.arange(0, MLA_ROPE, 2, dtype=F32) / MLA_ROPE)
    ang = pos[:, None] * inv[None, :]
    cos, sin = jnp.cos(ang), jnp.sin(ang)
    z_lo = jnp.zeros((S, MLA_NOPE), F32)
    z_hi = jnp.zeros((S, HEAD_PAD - MLA_NOPE - 2 * half), F32)
    ck = jnp.concatenate([z_lo, cos, cos, z_hi], axis=-1)
    sk = jnp.concatenate([z_lo, -sin, sin, z_hi], axis=-1)
    scale = math.log2(math.e) / math.sqrt(MLA_NOPE + MLA_ROPE)
    cq = jnp.concatenate([jnp.ones((S, MLA_NOPE), F32), cos, cos, z_hi], axis=-1) * scale
    sq = sk * scale
    return cq, sq, ck, sk


def _pad_heads(w, n_heads, width, keep):
    K = w.shape[0]
    w = w.reshape(K, n_heads, width)[:, :, :keep]
    w = jnp.pad(w, ((0, 0), (0, 0), (0, HEAD_PAD - keep)))
    return w.reshape(K, n_heads * HEAD_PAD)


def _layer_weights(w_in, w_uq, w_ukv):
    D = w_in.shape[0]
    o = MLA_Q_RANK + MLA_KV_RANK
    w_kpe = w_in[:, o:o + MLA_ROPE]
    zl = jnp.zeros((D, MLA_NOPE), F32)
    zh = jnp.zeros((D, HEAD_PAD - MLA_NOPE - MLA_ROPE), F32)
    half = MLA_ROPE // 2
    kpe_pad = jnp.concatenate([zl, w_kpe, zh], axis=-1)
    wlat = jnp.concatenate([w_in[:, :o], kpe_pad], axis=-1).astype(BF16)
    o2 = o + MLA_ROPE
    why = w_in[:, o2:o2 + 3 * HY_WIDTH].astype(BF16)
    wna = w_in[:, o2 + 3 * HY_WIDTH:].astype(BF16)
    qd = MLA_NOPE + MLA_ROPE
    wq = _pad_heads(w_uq, MLA_HEADS, qd, qd).astype(BF16)
    kvd = MLA_NOPE + MLA_V
    wk = _pad_heads(w_ukv, MLA_HEADS, kvd, MLA_NOPE).astype(BF16)
    wv = w_ukv.reshape(-1, MLA_HEADS, kvd)[:, :, MLA_NOPE:].reshape(-1, MLA_WIDTH)
    wvt = wv.T.astype(BF16)
    return wlat, why, wna, wq, wk, wvt


def _pad2(a, rows, cols):
    return jnp.pad(a, ((0, rows - a.shape[0]), (0, cols - a.shape[1])))


def kernel(x, norm1_g, w_in, mla_q_norm_g, mla_w_uq, mla_kv_norm_g, mla_w_ukv, hy_conv_w, hy_conv_b, hy_filt_w1, hy_filt_b1, hy_filt_freq1, hy_filt_w2, hy_filt_b2, hy_filt_freq2, hy_filt_w3, hy_skip, na_rpb, mix_norm_g, w_out, norm2_g, ffn_w_gate, ffn_w_up, ffn_w_down, final_norm_g):
    B, S, D = x.shape
    depth = w_in.shape[0]
    L = S
    tm = min(1024, S)
    tq = min(512, S)
    tkv = min(256, S)
    kv_unroll = max(1, min(8, S // tkv))
    t_dft = min(MXU_WIDTH, L // 2)
    nb = next(n for n in (4, 2, 1) if B % n == 0)
    na_rows = min(16, S // GRID_W)

    cq, sq, ck, sk = _rope_tables(S)
    tabs = _dft_tables(L)
    z_feat, decay = _hyena_features(L)
    row = lambda v: v.reshape(1, -1)

    for l in range(depth):
        wlat, why, wna, wq, wk, wvt = _layer_weights(w_in[l], mla_w_uq[l], mla_w_ukv[l])
        q, k, vt, hy, naq, nak, nav = _inproj(
            x, row(norm1_g[l]), wlat, why, wna, row(mla_q_norm_g[l]), wq,
            row(mla_kv_norm_g[l]), wk, wvt, cq, sq, ck, sk, tm=tm)

        y_a = _mla_attn(q, k, vt, tq=tq, tk=tkv, unroll=kv_unroll)
        y_c = _natten(naq, nak, nav, _natten_bias_pairs(na_rpb[l]), G=na_rows)

        u, g1, g2 = _short_conv(hy, hy_conv_w[l], row(hy_conv_b[l]), B)
        hp, hm = _filters(
            z_feat, _pad2(hy_filt_w1[l], LANE, LANE), _pad2(row(hy_filt_b1[l]), 1, LANE),
            _pad2(row(hy_filt_freq1[l]), 1, LANE), _pad2(hy_filt_w2[l], LANE, LANE),
            _pad2(row(hy_filt_b2[l]), 1, LANE), _pad2(row(hy_filt_freq2[l]), 1, LANE),
            _pad2(hy_filt_w3[l], LANE, hy_filt_w3.shape[-1]), decay, tm=tm)
        halves = lambda a: a.reshape(2, L // 2, a.shape[-1])
        coefs = _coefs(tabs, halves(hp), halves(hm), row(hy_skip[l]), tk=t_dft)
        xs = _fwd_dft(tabs, u, coefs, 0, tk=t_dft, nb=nb)
        z1 = _inv_dft(tabs, xs, g1, BF16, tm=t_dft, nb=nb)
        xs = _fwd_dft(tabs, z1, coefs, 1, tk=t_dft, nb=nb)
        y_b = _inv_dft(tabs, xs, g2, BF16, tm=t_dft, nb=nb)

        x = _mix_ffn(x, y_a, y_b, y_c, row(mix_norm_g[l]), w_out[l].astype(BF16),
                     row(norm2_g[l]), ffn_w_gate[l].astype(BF16), ffn_w_up[l].astype(BF16),
                     ffn_w_down[l].astype(BF16), row(final_norm_g),
                     tm=tm, final_norm=(l == depth - 1))
    return x
```

```python
import functools
import math

import jax
import jax.numpy as jnp
import numpy as np
from jax import lax
from jax.experimental import pallas as pl
from jax.experimental.pallas import tpu as pltpu

F32 = jnp.float32
BF16 = jnp.bfloat16

NORM_EPS = 1e-6
MLA_HEADS = 6
MLA_NOPE = 64
MLA_ROPE = 32
MLA_V = 64
MLA_Q_RANK = 256
MLA_KV_RANK = 128
ROPE_THETA = 10000.0
HY_WIDTH = 384
HY_ORDER = 2
HY_BANDS = 8
HY_EMB = 1 + 2 * HY_BANDS
HY_FFN = 64
HY_FAST_DECAY = 0.3
HY_SLOW_DECAY = 1.5
HY_TARGET = 1e-2
NA_HEADS = 4
NA_HEAD_DIM = 64
NA_WIDTH = NA_HEADS * NA_HEAD_DIM
GRID_W = 64
NA_KH = 8
NA_KW = 16
MLA_WIDTH = MLA_HEADS * MLA_V

LANE = 128
HEAD_PAD = 128
F32_SUBLANES = 8
BF16_SUBLANES = 16
MLA_VPAD = MLA_V + BF16_SUBLANES
VMEM_LIMIT = 56 * 1024 * 1024
MASK_VALUE = -1e30
MXU_WIDTH = 256
DFT_COL_CHUNK = MXU_WIDTH


def _params(sem, vmem=VMEM_LIMIT):
    return pltpu.CompilerParams(dimension_semantics=sem, vmem_limit_bytes=vmem)


def _rms(x, g):
    return x * lax.rsqrt(jnp.mean(x * x, axis=-1, keepdims=True) + NORM_EPS) * g


def _const_spec(shape):
    nd = len(shape)
    return pl.BlockSpec(shape, lambda *_: (0,) * nd)


def _dft_table_kernel(ce_ref, se_ref, co_ref, so_ref, cot_ref, sot_ref,
                      eec_ref, ees_ref, eoc_ref, eos_ref, *, L, tk):
    i = pl.program_id(0)
    n = 2 * L
    H = L // 2
    w = 2.0 * math.pi / n

    def angle(prod):
        return (prod & (n - 1)).astype(F32) * w

    @pl.when(i == 0)
    def _():
        r = lax.broadcasted_iota(jnp.int32, (tk, H), 0)
        c = lax.broadcasted_iota(jnp.int32, (tk, H), 1)
        ang_e = angle(r * (2 * c))
        ang_o = angle(r * (2 * c + 1))
        eec_ref[...] = jnp.cos(ang_e)
        ees_ref[...] = jnp.sin(ang_e)
        eoc_ref[...] = jnp.cos(ang_o)
        eos_ref[...] = jnp.sin(ang_o)

    k0 = i * tk
    c1 = lax.broadcasted_iota(jnp.int32, (1, H), 1)

    def rotate(a0, ec_ref, es_ref):
        ca, sa = jnp.cos(a0), jnp.sin(a0)
        ec, es = ec_ref[...], es_ref[...]
        return (ca * ec - sa * es).astype(BF16), (sa * ec + ca * es).astype(BF16)

    ce_ref[...], se_ref[...] = rotate(angle(k0 * (2 * c1)), eec_ref, ees_ref)
    co_ref[...], so_ref[...] = rotate(angle(k0 * (2 * c1 + 1)), eoc_ref, eos_ref)
    cot_ref[...], sot_ref[...] = rotate(angle(c1 * (2 * k0 + 1)), eec_ref, ees_ref)


def _dft_tables(L):
    assert L & (L - 1) == 0, "sequence length must be a power of two"
    H = L // 2
    tk = min(128, H)
    spec = pl.BlockSpec((tk, H), lambda i: (i, 0))
    shp = jax.ShapeDtypeStruct((H, H), BF16)
    return pl.pallas_call(
        functools.partial(_dft_table_kernel, L=L, tk=tk),
        grid=(H // tk,),
        out_specs=[spec] * 6,
        out_shape=[shp] * 6,
        scratch_shapes=[pltpu.VMEM((tk, H), F32)] * 4,
        compiler_params=_params(("arbitrary",)),
        name="dft_tables",
    )()


def _swap_rope_halves(x):
    half = MLA_ROPE // 2
    lane = lax.broadcasted_iota(jnp.int32, (1, x.shape[1]), 1) % HEAD_PAD
    first = (lane >= MLA_NOPE) & (lane < MLA_NOPE + half)
    width = x.shape[1]
    return jnp.where(first, pltpu.roll(x, width - half, axis=1), pltpu.roll(x, half, axis=1))


def _inproj_kernel(x_ref, g1_ref, wlat_ref, why_ref, wna_ref, gq_ref, wq_ref,
                   gkv_ref, wk_ref, wvt_ref, cq_ref, sq_ref, ck_ref, sk_ref,
                   q_ref, k_ref, vt_ref, hv_ref, hx1_ref, hx2_ref, naq_ref, nak_ref, nav_ref,
                   hy_scr):
    x = x_ref[0]
    h = _rms(x, g1_ref[...]).astype(BF16)
    lat = jnp.dot(h, wlat_ref[...], preferred_element_type=F32)
    hy = jnp.dot(h, why_ref[...], preferred_element_type=F32)
    half = hy.shape[0] // 2
    nj = HY_WIDTH // LANE
    for j in range(hy.shape[1] // LANE):
        o_ref = (hv_ref, hx1_ref, hx2_ref)[j // nj]
        sl = slice((j % nj) * LANE, (j % nj + 1) * LANE)
        hy_scr[j] = hy[:, j * LANE:(j + 1) * LANE]
        o_ref[0, :, sl] = hy_scr[j, pl.ds(0, half, stride=2), :].astype(BF16)
        o_ref[1, :, sl] = hy_scr[j, pl.ds(1, half, stride=2), :].astype(BF16)
    na = jnp.dot(h, wna_ref[...], preferred_element_type=F32)
    naq_ref[0] = (na[:, :NA_WIDTH] * (1.0 / math.sqrt(NA_HEAD_DIM))).astype(BF16)
    nak_ref[0] = na[:, NA_WIDTH:2 * NA_WIDTH].astype(BF16)
    nav_ref[0] = na[:, 2 * NA_WIDTH:].astype(BF16)

    c_q = lat[:, :MLA_Q_RANK]
    c_kv = lat[:, MLA_Q_RANK:MLA_Q_RANK + MLA_KV_RANK]
    o = MLA_Q_RANK + MLA_KV_RANK
    kpe = lat[:, o:o + HEAD_PAD]

    cqn = _rms(c_q, gq_ref[...]).astype(BF16)
    qf = jnp.dot(cqn, wq_ref[...], preferred_element_type=F32)
    qs = _swap_rope_halves(qf)
    cq = cq_ref[...]
    sq = sq_ref[...]
    ckvn = _rms(c_kv, gkv_ref[...])
    kf = jnp.dot(ckvn.astype(BF16), wk_ref[...], preferred_element_type=F32)
    kpe_r = kpe * ck_ref[...] + _swap_rope_halves(kpe) * sk_ref[...]
    for hd in range(MLA_HEADS):
        sl = slice(hd * HEAD_PAD, (hd + 1) * HEAD_PAD)
        q_ref[0, :, sl] = (qf[:, sl] * cq + qs[:, sl] * sq).astype(BF16)
        k_ref[0, :, sl] = (kf[:, sl] + kpe_r).astype(BF16)
    vt = jnp.dot(wvt_ref[...], ckvn.T.astype(BF16), preferred_element_type=F32).astype(BF16)
    ones = jnp.ones((MLA_VPAD - MLA_V, vt.shape[1]), BF16)
    for hd in range(MLA_HEADS):
        vt_ref[0, hd * MLA_VPAD:hd * MLA_VPAD + MLA_V, :] = vt[hd * MLA_V:(hd + 1) * MLA_V]
        vt_ref[0, hd * MLA_VPAD + MLA_V:(hd + 1) * MLA_VPAD, :] = ones


def _inproj(x, g1, wlat, why, wna, gq, wq, gkv, wk, wvt, cq, sq, ck, sk, *, tm):
    B, S, D = x.shape
    ns = S // tm
    hyw = why.shape[1]
    qw = MLA_HEADS * HEAD_PAD
    tok = lambda w: pl.BlockSpec((1, tm, w), lambda i: (i // ns, i % ns, 0))
    tab = pl.BlockSpec((tm, HEAD_PAD), lambda i: (i % ns, 0))
    hseg = pl.BlockSpec((2, tm // 2, HY_WIDTH), lambda i: (0, i % ns, i // ns))
    hshape = jax.ShapeDtypeStruct((2, S // 2, B * HY_WIDTH), BF16)
    in_specs = [tok(D), _const_spec(g1.shape), _const_spec(wlat.shape), _const_spec(why.shape),
                _const_spec(wna.shape), _const_spec(gq.shape), _const_spec(wq.shape),
                _const_spec(gkv.shape), _const_spec(wk.shape),
                _const_spec(wvt.shape), tab, tab, tab, tab]
    out_specs = [tok(qw), tok(qw),
                 pl.BlockSpec((1, MLA_HEADS * MLA_VPAD, tm), lambda i: (i // ns, 0, i % ns)),
                 hseg, hseg, hseg,
                 tok(NA_WIDTH), tok(NA_WIDTH), tok(NA_WIDTH)]
    out_shape = [jax.ShapeDtypeStruct((B, S, qw), BF16),
                 jax.ShapeDtypeStruct((B, S, qw), BF16),
                 jax.ShapeDtypeStruct((B, MLA_HEADS * MLA_VPAD, S), BF16),
                 hshape, hshape, hshape,
                 jax.ShapeDtypeStruct((B, S, NA_WIDTH), BF16),
                 jax.ShapeDtypeStruct((B, S, NA_WIDTH), BF16),
                 jax.ShapeDtypeStruct((B, S, NA_WIDTH), BF16)]
    return pl.pallas_call(
        _inproj_kernel, grid=(B * ns,), in_specs=in_specs, out_specs=out_specs,
        out_shape=out_shape, scratch_shapes=[pltpu.VMEM((hyw // LANE, tm, LANE), F32)],
        compiler_params=_params(("parallel",)), name="inproj",
    )(x, g1, wlat, why, wna, gq, wq, gkv, wk, wvt, cq, sq, ck, sk)


def _mla_attn_kernel(q_ref, k_ref, vt_ref, o_ref, m_ref, acc_ref, *, tk, unroll):
    S = k_ref.shape[1]
    nk = S // tk
    m_ref[...] = jnp.full(m_ref.shape, -jnp.inf, F32)
    acc_ref[...] = jnp.zeros(acc_ref.shape, F32)

    def body(j, carry):
        base = j * (unroll * tk)

        def scores(c, hd):
            off = pl.multiple_of(base + c * tk, tk)
            sl = slice(hd * HEAD_PAD, (hd + 1) * HEAD_PAD)
            kj = k_ref[0, pl.ds(off, tk), sl]
            s = lax.dot_general(kj, q_ref[0, :, sl], (((1,), (1,)), ((), ())),
                                preferred_element_type=F32)
            m_old = m_ref[hd]
            m_new = jnp.maximum(m_old, jnp.max(s, axis=0, keepdims=True))
            m_ref[hd] = m_new
            return s, m_old, m_new

        def probs(s, m_old, m_new):
            return jnp.exp2(s - m_new).astype(BF16), jnp.exp2(m_old - m_new)

        def accumulate(c, hd, p, alpha):
            off = pl.multiple_of(base + c * tk, tk)
            vj = vt_ref[0, hd * MLA_VPAD:(hd + 1) * MLA_VPAD, pl.ds(off, tk)]
            acc_ref[hd] = alpha * acc_ref[hd] + jnp.dot(vj, p, preferred_element_type=F32)

        items = [(c, hd) for c in range(unroll) for hd in range(MLA_HEADS)]
        n = len(items)
        st_scores, st_probs = {}, {}
        for t in range(n + 2):
            if t < n:
                st_scores[t] = scores(*items[t])
            if 0 <= t - 1 < n:
                st_probs[t - 1] = probs(*st_scores.pop(t - 1))
            if 0 <= t - 2 < n:
                accumulate(*items[t - 2], *st_probs.pop(t - 2))
        return carry

    lax.fori_loop(0, nk // unroll, body, 0)
    outs = []
    for hd in range(MLA_HEADS):
        acc = acc_ref[hd]
        outs.append(acc[:MLA_V] / acc[MLA_V:MLA_V + 1])
    o_ref[0] = jnp.concatenate(outs, axis=0).T.astype(o_ref.dtype)


def _mla_attn(q, k, vt, *, tq, tk, unroll):
    B, S, qw = q.shape
    vrows = vt.shape[1]
    assert S % (tk * unroll) == 0
    return pl.pallas_call(
        functools.partial(_mla_attn_kernel, tk=tk, unroll=unroll),
        grid=(B, S // tq),
        in_specs=[pl.BlockSpec((1, tq, qw), lambda b, i: (b, i, 0)),
                  pl.BlockSpec((1, S, qw), lambda b, i: (b, 0, 0)),
                  pl.BlockSpec((1, vrows, S), lambda b, i: (b, 0, 0))],
        out_specs=pl.BlockSpec((1, tq, MLA_WIDTH), lambda b, i: (b, i, 0)),
        out_shape=jax.ShapeDtypeStruct((B, S, MLA_WIDTH), BF16),
        scratch_shapes=[pltpu.VMEM((MLA_HEADS, 1, tq), F32),
                        pltpu.VMEM((MLA_HEADS, MLA_VPAD, tq), F32)],
        compiler_params=_params(("parallel", "arbitrary")), name="mla_attn",
    )(q, k, vt)


def _natten_kernel(q_ref, k_ref, v_ref, bias_ref, o_ref, *, R, KH, G):
    col_head = lax.broadcasted_iota(jnp.int32, (1, NA_WIDTH), 1) // NA_HEAD_DIM
    sels = [col_head == hd for hd in range(NA_HEADS)]

    def window(i):
        r = pl.program_id(1) * G + i
        start = jnp.clip(r - KH // 2, 0, R - KH)
        return start - r + (NA_KH - 1), pl.multiple_of(start * GRID_W, GRID_W)

    def scores(i):
        dr0, off = window(i)
        q = q_ref[0, i * GRID_W:(i + 1) * GRID_W, :]
        qm = jnp.concatenate([jnp.where(sel, q, jnp.zeros_like(q)) for sel in sels], axis=0)
        kw = k_ref[0, pl.ds(off, KH * GRID_W), :]
        s = lax.dot_general(qm, kw, (((1,), (1,)), ((), ())),
                            preferred_element_type=F32)
        bias = jnp.concatenate([bias_ref[dr0 + kh] for kh in range(0, KH, 2)], axis=-1)
        logits = s + bias
        return logits, jnp.max(logits, axis=-1, keepdims=True)

    def probs(logits, m):
        p = jnp.exp(logits - m)
        return p.astype(BF16), jnp.sum(p, axis=-1, keepdims=True)

    def output(i, p, l):
        _, off = window(i)
        vw = v_ref[0, pl.ds(off, KH * GRID_W), :]
        o = jnp.dot(p, vw, preferred_element_type=F32) / l
        y = jnp.zeros((GRID_W, NA_WIDTH), F32)
        for hd, sel in enumerate(sels):
            y = y + jnp.where(sel, o[hd * GRID_W:(hd + 1) * GRID_W], 0.0)
        o_ref[0, i * GRID_W:(i + 1) * GRID_W, :] = y.astype(o_ref.dtype)

    st_scores, st_probs = {}, {}
    for t in range(G + 2):
        if t < G:
            st_scores[t] = scores(t)
        if 0 <= t - 1 < G:
            st_probs[t - 1] = probs(*st_scores.pop(t - 1))
        if 0 <= t - 2 < G:
            output(t - 2, *st_probs.pop(t - 2))


def _natten(q, k, v, bias_pairs, *, G):
    B, S, _ = q.shape
    R = S // GRID_W
    KH = min(NA_KH, R)
    assert KH % 2 == 0 and R % G == 0
    return pl.pallas_call(
        functools.partial(_natten_kernel, R=R, KH=KH, G=G),
        grid=(B, R // G),
        in_specs=[pl.BlockSpec((1, G * GRID_W, NA_WIDTH), lambda b, r: (b, r, 0)),
                  pl.BlockSpec((1, S, NA_WIDTH), lambda b, r: (b, 0, 0)),
                  pl.BlockSpec((1, S, NA_WIDTH), lambda b, r: (b, 0, 0)),
                  _const_spec(bias_pairs.shape)],
        out_specs=pl.BlockSpec((1, G * GRID_W, NA_WIDTH), lambda b, r: (b, r, 0)),
        out_shape=jax.ShapeDtypeStruct((B, S, NA_WIDTH), BF16),
        compiler_params=_params(("parallel", "arbitrary")), name="natten",
    )(q, k, v, bias_pairs)


def _natten_bias_pairs(rpb):
    c = np.arange(GRID_W)
    start = np.clip(c - NA_KW // 2, 0, GRID_W - NA_KW)
    v = c[None, :]
    inwin = (v >= start[:, None]) & (v < start[:, None] + NA_KW)
    dc = v - c[:, None] + (NA_KW - 1)
    onehot = (dc[:, :, None] == np.arange(2 * NA_KW - 1)[None, None, :]) & inwin[:, :, None]
    t = jnp.einsum('hdj,wvj->hdwv', rpb.astype(F32), jnp.asarray(onehot, F32),
                   precision=lax.Precision.HIGHEST)
    t = jnp.where(jnp.asarray(inwin)[None, None], t, MASK_VALUE)
    t = jnp.concatenate([t[:, :-1], t[:, 1:]], axis=-1).astype(F32)
    return t.transpose(1, 0, 2, 3).reshape(t.shape[1], NA_HEADS * GRID_W, 2 * GRID_W)


def _alt_sign_sum(x):
    rows, cols = x.shape
    if rows % F32_SUBLANES == 0:
        x = jnp.sum(x.astype(F32).reshape(rows // F32_SUBLANES, F32_SUBLANES, cols), axis=0)
    odd = (lax.broadcasted_iota(jnp.int32, x.shape, 0) & 1) == 1
    xf = x.astype(F32)
    return jnp.sum(jnp.where(odd, -xf, xf), axis=0, keepdims=True)


def _conv3_deinterleaved(xe, xo, w, b):
    H = xe.shape[0]
    i = lax.broadcasted_iota(jnp.int32, xe.shape, 0)
    xo_prev = jnp.where(i == 0, 0.0, pltpu.roll(xo, 1, axis=0))
    xe_next = jnp.where(i == H - 1, 0.0, pltpu.roll(xe, H - 1, axis=0))
    ye = w[0:1] * xo_prev + w[1:2] * xe + w[2:3] * xo + b
    yo = w[0:1] * xe + w[1:2] * xo + w[2:3] * xe_next + b
    return ye, yo


def _short_conv_kernel(x1_ref, x2_ref, w1_ref, w2_ref, b1_ref, b2_ref, g1_ref, g2_ref):
    for x_ref, w_ref, b_ref, o_ref in ((x1_ref, w1_ref, b1_ref, g1_ref),
                                       (x2_ref, w2_ref, b2_ref, g2_ref)):
        ye, yo = _conv3_deinterleaved(x_ref[0].astype(F32), x_ref[1].astype(F32),
                                      w_ref[...], b_ref[...])
        o_ref[0] = ye.astype(o_ref.dtype)
        o_ref[1] = yo.astype(o_ref.dtype)


def _short_conv(hx1, hx2, conv_w, conv_b, B):
    _, H, _ = hx1.shape
    C = HY_WIDTH
    nj = C // LANE
    blk = pl.BlockSpec((2, H, LANE), lambda b, j: (0, 0, b * nj + j))
    wseg = lambda s: pl.BlockSpec((3, LANE), lambda b, j: (0, s * nj + j))
    bseg = lambda s: pl.BlockSpec((1, LANE), lambda b, j: (0, s * nj + j))
    return pl.pallas_call(
        _short_conv_kernel, grid=(B, nj),
        in_specs=[blk, blk, wseg(1), wseg(2), bseg(1), bseg(2)],
        out_specs=[blk, blk],
        out_shape=[jax.ShapeDtypeStruct((2, H, B * C), BF16)] * 2,
        compiler_params=_params(("parallel", "parallel")), name="hy_short_conv",
    )(hx1, hx2, conv_w, conv_w, conv_b, conv_b)


def _filter_kernel(z_ref, w1_ref, b1_ref, f1_ref, w2_ref, b2_ref, f2_ref, w3_ref, dec_ref,
                   hp_ref, hm_ref):
    hi = lax.Precision.HIGHEST
    dot = lambda a, b: jnp.dot(a, b, precision=hi, preferred_element_type=F32)
    h = jnp.sin(f1_ref[...] * (dot(z_ref[...], w1_ref[...]) + b1_ref[...]))
    h = jnp.sin(f2_ref[...] * (dot(h, w2_ref[...]) + b2_ref[...]))
    h = jnp.dot(h.astype(BF16), w3_ref[...].astype(BF16),
                preferred_element_type=F32)
    dec = dec_ref[...]
    C = HY_WIDTH
    for o in range(HY_ORDER):
        hf = h[:, (2 * o) * C:(2 * o + 1) * C] * dec
        hb = h[:, (2 * o + 1) * C:(2 * o + 2) * C] * dec
        hp_ref[:, o * C:(o + 1) * C] = (hf + hb).astype(BF16)
        hm_ref[:, o * C:(o + 1) * C] = (hf - hb).astype(BF16)


def _filters(z, w1, b1, f1, w2, b2, f2, w3, decay, *, tm):
    L = z.shape[0]
    OC = HY_ORDER * HY_WIDTH
    row = lambda w: pl.BlockSpec((tm, w), lambda i: (i, 0))
    return pl.pallas_call(
        _filter_kernel, grid=(L // tm,),
        in_specs=[row(z.shape[1]), _const_spec(w1.shape), _const_spec(b1.shape),
                  _const_spec(f1.shape), _const_spec(w2.shape), _const_spec(b2.shape),
                  _const_spec(f2.shape), _const_spec(w3.shape), row(HY_WIDTH)],
        out_specs=[row(OC), row(OC)],
        out_shape=[jax.ShapeDtypeStruct((L, OC), BF16), jax.ShapeDtypeStruct((L, OC), BF16)],
        compiler_params=_params(("parallel",)), name="hy_filters",
    )(z, w1, b1, f1, w2, b2, f2, w3, decay)


def _coef_kernel(ce_ref, se_ref, co_ref, so_ref, hp_ref, hm_ref, skip_ref,
                 alo_ref, ahi_ref, blo_ref, bhi_ref, ah_ref, bh_ref, *, L):
    i = pl.program_id(0)
    tk = ce_ref.shape[0]
    inv = 1.0 / L
    skip = skip_ref[...]
    pe = jnp.dot(ce_ref[...], hp_ref[0], preferred_element_type=F32)
    po = jnp.dot(co_ref[...], hp_ref[1], preferred_element_type=F32)
    qe = jnp.dot(se_ref[...], hm_ref[0], preferred_element_type=F32)
    qo = jnp.dot(so_ref[...], hm_ref[1], preferred_element_type=F32)
    is0 = (i * tk + lax.broadcasted_iota(jnp.int32, pe.shape, 0)) == 0
    sc = jnp.where(is0, 0.5 * inv, inv)
    alo_ref[...] = sc * (pe + po + skip)
    ahi_ref[...] = sc * (pe - po + skip)
    blo_ref[...] = -inv * (qe + qo)
    bhi_ref[...] = inv * (qe - qo)

    @pl.when(i == 0)
    def _():
        gr = _alt_sign_sum(hp_ref[0]) + skip
        gi = -_alt_sign_sum(hm_ref[1])
        ah_ref[...] = jnp.broadcast_to(inv * gr, ah_ref.shape)
        bh_ref[...] = jnp.broadcast_to(inv * gi, bh_ref.shape)


def _coefs(tabs, hp, hm, skip, *, tk):
    ce, se, co, so = tabs[:4]
    H = ce.shape[0]
    OC = hp.shape[-1]
    blk = pl.BlockSpec((tk, H), lambda i: (i, 0))
    out = pl.BlockSpec((tk, OC), lambda i: (i, 0))
    half = pl.BlockSpec((F32_SUBLANES, OC), lambda i: (0, 0))
    shp = jax.ShapeDtypeStruct((H, OC), F32)
    shp_h = jax.ShapeDtypeStruct((F32_SUBLANES, OC), F32)
    return pl.pallas_call(
        functools.partial(_coef_kernel, L=2 * H), grid=(H // tk,),
        in_specs=[blk, blk, blk, blk, _const_spec(hp.shape), _const_spec(hm.shape),
                  _const_spec(skip.shape)],
        out_specs=[out, out, out, out, half, half],
        out_shape=[shp, shp, shp, shp, shp_h, shp_h],
        compiler_params=_params(("arbitrary",)), name="hy_coefs",
    )(ce, se, co, so, hp, hm, skip)


def _fwd_dft_kernel(ce_ref, se_ref, co_ref, so_ref, u_ref, alo_ref, ahi_ref, blo_ref, bhi_ref,
                    ah_ref, bh_ref, xa_ref, xb_ref, xc_ref, xd_ref, r1h_ref, r2h_ref, *, nb):
    tile = lambda x: jnp.concatenate([x] * nb, axis=-1) if nb > 1 else x
    a_lo_t, a_hi_t = tile(alo_ref[...]), tile(ahi_ref[...])
    b_lo_t, b_hi_t = tile(blo_ref[...]), tile(bhi_ref[...])
    cb = u_ref.shape[-1]
    chunk = DFT_COL_CHUNK if cb % DFT_COL_CHUNK == 0 else LANE
    for c in range(cb // chunk):
        sl = slice(c * chunk, (c + 1) * chunk)
        ue = u_ref[0, :, sl]
        uo = u_ref[1, :, sl]
        pe = jnp.dot(ce_ref[...], ue, preferred_element_type=F32)
        po = jnp.dot(co_ref[...], uo, preferred_element_type=F32)
        qe = jnp.dot(se_ref[...], ue, preferred_element_type=F32)
        qo = jnp.dot(so_ref[...], uo, preferred_element_type=F32)
        p_lo, p_hi, q_lo, q_hi = pe + po, pe - po, qe + qo, qo - qe
        a_lo, a_hi, b_lo, b_hi = a_lo_t[:, sl], a_hi_t[:, sl], b_lo_t[:, sl], b_hi_t[:, sl]
        r1_lo = p_lo * a_lo + q_lo * b_lo
        r2_lo = q_lo * a_lo - p_lo * b_lo
        r1_hi = p_hi * a_hi + q_hi * b_hi
        r2_hi = q_hi * a_hi - p_hi * b_hi
        xa_ref[:, sl] = (r1_lo + r1_hi).astype(BF16)
        xb_ref[:, sl] = (r2_lo - r2_hi).astype(BF16)
        xc_ref[:, sl] = (r1_lo - r1_hi).astype(BF16)
        xd_ref[:, sl] = (r2_lo + r2_hi).astype(BF16)

    @pl.when(pl.program_id(1) == 0)
    def _():
        ph = _alt_sign_sum(u_ref[0])
        qh = _alt_sign_sum(u_ref[1])
        a_h = tile(ah_ref[0:1])
        b_h = tile(bh_ref[0:1])
        r1h_ref[...] = jnp.broadcast_to(ph * a_h + qh * b_h, r1h_ref.shape)
        r2h_ref[...] = jnp.broadcast_to(qh * a_h - ph * b_h, r2h_ref.shape)


def _fwd_dft_conv_kernel(ce_ref, se_ref, co_ref, so_ref, hv_ref, w_ref, b_ref, *rest, nb):
    *refs, u_scr = rest

    @pl.when(pl.program_id(1) == 0)
    def _():
        cb = hv_ref.shape[-1]
        chunk = DFT_COL_CHUNK if cb % DFT_COL_CHUNK == 0 else LANE
        for c in range(cb // chunk):
            sl = slice(c * chunk, (c + 1) * chunk)
            ye, yo = _conv3_deinterleaved(hv_ref[0, :, sl].astype(F32),
                                          hv_ref[1, :, sl].astype(F32),
                                          w_ref[:, sl], b_ref[:, sl])
            u_scr[0, :, sl] = ye.astype(BF16)
            u_scr[1, :, sl] = yo.astype(BF16)

    _fwd_dft_kernel(ce_ref, se_ref, co_ref, so_ref, u_scr, *refs, nb=nb)


def _fwd_dft(tabs, u, coefs, order, *, tk, nb, conv=None):
    ce, se, co, so = tabs[:4]
    alo, ahi, blo, bhi, ah, bh = coefs
    H = ce.shape[0]
    C = HY_WIDTH
    cb = nb * C
    cols = u.shape[-1]
    blk = pl.BlockSpec((tk, H), lambda c, j: (j, 0))
    coef = pl.BlockSpec((tk, C), lambda c, j: (j, order))
    half = pl.BlockSpec((F32_SUBLANES, C), lambda c, j: (0, order))
    out = pl.BlockSpec((tk, cb), lambda c, j: (j, c))
    out_h = pl.BlockSpec((F32_SUBLANES, cb), lambda c, j: (0, c))
    shp = jax.ShapeDtypeStruct((H, cols), BF16)
    shp_h = jax.ShapeDtypeStruct((F32_SUBLANES, cols), F32)
    u_spec = pl.BlockSpec((2, H, cb), lambda c, j: (0, 0, c), pipeline_mode=pl.Buffered(1))
    common = dict(
        grid=(cols // cb, H // tk), out_specs=[out, out, out, out, out_h, out_h],
        out_shape=[shp, shp, shp, shp, shp_h, shp_h],
        compiler_params=_params(("parallel", "arbitrary")))
    if conv is None:
        return pl.pallas_call(
            functools.partial(_fwd_dft_kernel, nb=nb),
            in_specs=[blk, blk, blk, blk, u_spec, coef, coef, coef, coef, half, half],
            name="hy_fwd_dft", **common,
        )(ce, se, co, so, u, alo, ahi, blo, bhi, ah, bh)
    w, b = (jnp.tile(a, (1, nb)) for a in conv)
    return pl.pallas_call(
        functools.partial(_fwd_dft_conv_kernel, nb=nb),
        in_specs=[blk, blk, blk, blk, u_spec, _const_spec(w.shape), _const_spec(b.shape),
                  coef, coef, coef, coef, half, half],
        scratch_shapes=[pltpu.VMEM((2, H, cb), BF16)],
        name="hy_fwd_dft_conv", **common,
    )(ce, se, co, so, u, w, b, alo, ahi, blo, bhi, ah, bh)


def _inv_dft_kernel(ce_ref, se_ref, cot_ref, sot_ref, xa_ref, xb_ref, xc_ref, xd_ref,
                    r1h_ref, r2h_ref, g_ref, z_ref):
    tm = ce_ref.shape[0]
    ye = (jnp.dot(ce_ref[...], xa_ref[...], preferred_element_type=F32)
          + jnp.dot(se_ref[...], xb_ref[...], preferred_element_type=F32))
    yo = (jnp.dot(cot_ref[...], xc_ref[...], preferred_element_type=F32)
          + jnp.dot(sot_ref[...], xd_ref[...], preferred_element_type=F32))
    rows = pl.program_id(1) * tm + lax.broadcasted_iota(jnp.int32, ye.shape, 0)
    odd = (rows & 1) == 1
    r1h = r1h_ref[0:1]
    r2h = r2h_ref[0:1]
    ye = ye + jnp.where(odd, -r1h, r1h)
    yo = yo + jnp.where(odd, -r2h, r2h)
    z_ref[0] = (g_ref[0].astype(F32) * ye).astype(z_ref.dtype)
    z_ref[1] = (g_ref[1].astype(F32) * yo).astype(z_ref.dtype)


def _inv_dft(tabs, xs, gate, out_dtype, *, tm, nb):
    ce, se, _, _, cot, sot = tabs
    xa, xb, xc, xd, r1h, r2h = xs
    H = ce.shape[0]
    cb = nb * HY_WIDTH
    cols = xa.shape[-1]
    blk = pl.BlockSpec((tm, H), lambda c, i: (i, 0))
    full = pl.BlockSpec((H, cb), lambda c, i: (0, c), pipeline_mode=pl.Buffered(1))
    half = pl.BlockSpec((F32_SUBLANES, cb), lambda c, i: (0, c))
    tile = pl.BlockSpec((2, tm, cb), lambda c, i: (0, i, c))
    return pl.pallas_call(
        _inv_dft_kernel, grid=(cols // cb, H // tm),
        in_specs=[blk, blk, blk, blk, full, full, full, full, half, half, tile],
        out_specs=tile, out_shape=jax.ShapeDtypeStruct((2, H, cols), out_dtype),
        compiler_params=_params(("parallel", "arbitrary")), name="hy_inv_dft",
    )(ce, se, cot, sot, xa, xb, xc, xd, r1h, r2h, gate)


def _hyena_features(L):
    deint = lambda a: jnp.concatenate([a[0::2], a[1::2]], axis=0)
    t_idx = deint(jnp.arange(L, dtype=F32))[:, None]
    t_norm = deint(jnp.linspace(0.0, 1.0, L, dtype=F32))[:, None]
    bands = jnp.linspace(1e-4, HY_BANDS - 1, HY_BANDS, dtype=F32)[None, :]
    ang = 2.0 * math.pi * t_idx * bands / L
    z = jnp.concatenate([t_norm, jnp.cos(ang), jnp.sin(ang)], axis=-1)
    z = jnp.pad(z, ((0, 0), (0, LANE - HY_EMB)))
    deltas = jnp.linspace(math.log(HY_TARGET) / HY_SLOW_DECAY,
                          math.log(HY_TARGET) / HY_FAST_DECAY, HY_WIDTH, dtype=F32)
    decay = jnp.exp(-t_norm * jnp.abs(deltas)[None, :])
    return z, decay


def _mix_ffn_kernel(x_ref, ya_ref, yb_ref, yc_ref, gmix_ref, wout_ref, g2_ref, wg_ref, wu_ref,
                    wd_ref, gf_ref, o_ref, yb_scr, *, chunks, final_norm):
    gmix = gmix_ref[...]
    ca = ya_ref.shape[-1]
    cb = yb_ref.shape[-1]
    half = yb_ref.shape[1]
    for j in range(cb // LANE):
        sl = slice(j * LANE, (j + 1) * LANE)
        yb_scr[j, pl.ds(0, half, stride=2), :] = yb_ref[0, :, sl].astype(F32)
        yb_scr[j, pl.ds(1, half, stride=2), :] = yb_ref[1, :, sl].astype(F32)
    yb = jnp.concatenate([yb_scr[j] for j in range(cb // LANE)], axis=-1)
    ymix = jnp.concatenate([
        _rms(ya_ref[0].astype(F32), gmix[:, :ca]),
        _rms(yb, gmix[:, ca:ca + cb]),
        _rms(yc_ref[0].astype(F32), gmix[:, ca + cb:]),
    ], axis=-1).astype(BF16)
    x = x_ref[0] + jnp.dot(ymix, wout_ref[...], preferred_element_type=F32)
    h2 = _rms(x, g2_ref[...]).astype(BF16)
    acc = x
    for lo, hi in chunks:
        sl = slice(lo, hi)
        gate = jnp.dot(h2, wg_ref[:, sl], preferred_element_type=F32)
        up = jnp.dot(h2, wu_ref[:, sl], preferred_element_type=F32)
        act = (gate * jax.nn.sigmoid(gate) * up).astype(BF16)
        acc = acc + jnp.dot(act, wd_ref[sl, :], preferred_element_type=F32)
    if final_norm:
        acc = _rms(acc, gf_ref[...])
    o_ref[0] = acc


def _mix_ffn(x, ya, yb, yc, gmix, wout, g2, wg, wu, wd, gf, *, tm, final_norm):
    B, S, D = x.shape
    ns = S // tm
    tok = lambda w: pl.BlockSpec((1, tm, w), lambda i: (i // ns, i % ns, 0))
    once = lambda a: pl.BlockSpec(a.shape, lambda i: (0,) * a.ndim,
                                  pipeline_mode=pl.Buffered(1))
    ff = wg.shape[1]
    cut = pl.cdiv(pl.cdiv(ff, MXU_WIDTH), 2) * MXU_WIDTH
    chunks = ((0, cut), (cut, ff)) if cut < ff else ((0, ff),)
    return pl.pallas_call(
        functools.partial(_mix_ffn_kernel, chunks=chunks, final_norm=final_norm),
        grid=(B * ns,),
        in_specs=[tok(D), tok(ya.shape[-1]),
                  pl.BlockSpec((2, tm // 2, HY_WIDTH), lambda i: (0, i % ns, i // ns)),
                  tok(yc.shape[-1]), once(gmix), once(wout), once(g2), once(wg), once(wu),
                  once(wd), once(gf)],
        out_specs=tok(D), out_shape=jax.ShapeDtypeStruct((B, S, D), F32),
        scratch_shapes=[pltpu.VMEM((HY_WIDTH // LANE, tm, LANE), F32)],
        compiler_params=_params(("parallel",)), name="mix_ffn",
    )(x, ya, yb, yc, gmix, wout, g2, wg, wu, wd, gf)


def _rope_tables(S):
    half = MLA_ROPE // 2
    pos = jnp.arange(S, dtype=F32)
    inv = ROPE_THETA ** (-jnp.arange(0, MLA_ROPE, 2, dtype=F32) / MLA_ROPE)
    ang = pos[:, None] * inv[None, :]
    cos, sin = jnp.cos(ang), jnp.sin(ang)
    z_lo = jnp.zeros((S, MLA_NOPE), F32)
    z_hi = jnp.zeros((S, HEAD_PAD - MLA_NOPE - 2 * half), F32)
    ck = jnp.concatenate([z_lo, cos, cos, z_hi], axis=-1)
    sk = jnp.concatenate([z_lo, -sin, sin, z_hi], axis=-1)
    scale = math.log2(math.e) / math.sqrt(MLA_NOPE + MLA_ROPE)
    cq = jnp.concatenate([jnp.ones((S, MLA_NOPE), F32), cos, cos, z_hi], axis=-1) * scale
    sq = sk * scale
    return cq, sq, ck, sk


def _pad_heads(w, n_heads, width, keep):
    K = w.shape[0]
    w = w.reshape(K, n_heads, width)[:, :, :keep]
    w = jnp.pad(w, ((0, 0), (0, 0), (0, HEAD_PAD - keep)))
    return w.reshape(K, n_heads * HEAD_PAD)


def _layer_weights(w_in, w_uq, w_ukv):
    D = w_in.shape[0]
    o = MLA_Q_RANK + MLA_KV_RANK
    w_kpe = w_in[:, o:o + MLA_ROPE]
    zl = jnp.zeros((D, MLA_NOPE), F32)
    zh = jnp.zeros((D, HEAD_PAD - MLA_NOPE - MLA_ROPE), F32)
    half = MLA_ROPE // 2
    kpe_pad = jnp.concatenate([zl, w_kpe, zh], axis=-1)
    wlat = jnp.concatenate([w_in[:, :o], kpe_pad], axis=-1).astype(BF16)
    o2 = o + MLA_ROPE
    why = w_in[:, o2:o2 + 3 * HY_WIDTH].astype(BF16)
    wna = w_in[:, o2 + 3 * HY_WIDTH:].astype(BF16)
    qd = MLA_NOPE + MLA_ROPE
    wq = _pad_heads(w_uq, MLA_HEADS, qd, qd).astype(BF16)
    kvd = MLA_NOPE + MLA_V
    wk = _pad_heads(w_ukv, MLA_HEADS, kvd, MLA_NOPE).astype(BF16)
    wv = w_ukv.reshape(-1, MLA_HEADS, kvd)[:, :, MLA_NOPE:].reshape(-1, MLA_WIDTH)
    wvt = wv.T.astype(BF16)
    return wlat, why, wna, wq, wk, wvt


def _pad2(a, rows, cols):
    return jnp.pad(a, ((0, rows - a.shape[0]), (0, cols - a.shape[1])))


def kernel(x, norm1_g, w_in, mla_q_norm_g, mla_w_uq, mla_kv_norm_g, mla_w_ukv, hy_conv_w, hy_conv_b, hy_filt_w1, hy_filt_b1, hy_filt_freq1, hy_filt_w2, hy_filt_b2, hy_filt_freq2, hy_filt_w3, hy_skip, na_rpb, mix_norm_g, w_out, norm2_g, ffn_w_gate, ffn_w_up, ffn_w_down, final_norm_g):
    B, S, D = x.shape
    depth = w_in.shape[0]
    L = S
    tm = min(1024, S)
    tq = min(512, S)
    tkv = min(256, S)
    kv_unroll = max(1, min(8, S // tkv))
    t_dft = min(MXU_WIDTH, L // 2)
    nb = next(n for n in (4, 2, 1) if B % n == 0)
    na_rows = min(16, S // GRID_W)

    cq, sq, ck, sk = _rope_tables(S)
    tabs = _dft_tables(L)
    z_feat, decay = _hyena_features(L)
    row = lambda v: v.reshape(1, -1)

    for l in range(depth):
        wlat, why, wna, wq, wk, wvt = _layer_weights(w_in[l], mla_w_uq[l], mla_w_ukv[l])
        q, k, vt, hv, hx1, hx2, naq, nak, nav = _inproj(
            x, row(norm1_g[l]), wlat, why, wna, row(mla_q_norm_g[l]), wq,
            row(mla_kv_norm_g[l]), wk, wvt, cq, sq, ck, sk, tm=tm)

        y_a = _mla_attn(q, k, vt, tq=tq, tk=tkv, unroll=kv_unroll)
        y_c = _natten(naq, nak, nav, _natten_bias_pairs(na_rpb[l]), G=na_rows)

        conv_b = row(hy_conv_b[l])
        g1, g2 = _short_conv(hx1, hx2, hy_conv_w[l], conv_b, B)
        conv_v = (hy_conv_w[l][:, :HY_WIDTH], conv_b[:, :HY_WIDTH])
        hp, hm = _filters(
            z_feat, _pad2(hy_filt_w1[l], LANE, LANE), _pad2(row(hy_filt_b1[l]), 1, LANE),
            _pad2(row(hy_filt_freq1[l]), 1, LANE), _pad2(hy_filt_w2[l], LANE, LANE),
            _pad2(row(hy_filt_b2[l]), 1, LANE), _pad2(row(hy_filt_freq2[l]), 1, LANE),
            _pad2(hy_filt_w3[l], LANE, hy_filt_w3.shape[-1]), decay, tm=tm)
        halves = lambda a: a.reshape(2, L // 2, a.shape[-1])
        coefs = _coefs(tabs, halves(hp), halves(hm), row(hy_skip[l]), tk=t_dft)
        xs = _fwd_dft(tabs, hv, coefs, 0, tk=t_dft, nb=nb, conv=conv_v)
        z1 = _inv_dft(tabs, xs, g1, BF16, tm=t_dft, nb=nb)
        xs = _fwd_dft(tabs, z1, coefs, 1, tk=t_dft, nb=nb)
        y_b = _inv_dft(tabs, xs, g2, BF16, tm=t_dft, nb=nb)

        x = _mix_ffn(x, y_a, y_b, y_c, row(mix_norm_g[l]), w_out[l].astype(BF16),
                     row(norm2_g[l]), ffn_w_gate[l].astype(BF16), ffn_w_up[l].astype(BF16),
                     ffn_w_down[l].astype(BF16), row(final_norm_g),
                     tm=tm, final_norm=(l == depth - 1))
    return x
```
